```python
import math
import jax
import jax.numpy as jnp
from jax import lax
import numpy as np

D_MODEL = 1024
BATCH = 16
SEQ = 2048
DEPTH = 4

GRID_W = 64
CTX_LEN = 256
N_MIXERS = 3
N_A = (DEPTH + 2) // 3
N_B = (DEPTH + 1) // 3
N_C = DEPTH // 3
SC_WIDTH = 3
DA_HEADS = 8
DA_HEAD_DIM = D_MODEL // (2 * DA_HEADS)
DA_V_DIM = 2 * DA_HEAD_DIM
ROPE_FREQS = DA_HEAD_DIM // 4
ROPE_BASE = 10000.0
Q_BLOCK = 128
CF_WIDTH = 31
PEER_HEADS = 8
PEER_KEYS = 128
PEER_EXPERTS = PEER_KEYS * PEER_KEYS
PEER_QDIM = 256
PEER_HALF = PEER_QDIM // 2
PEER_TOPK = 16
PEER_CHUNK = 128
RMS_EPS = 1e-6
LN_EPS = 1e-5

kernel_name = 'hybrid_dit_shortconv_diffattn_conformer_peer'


def rms_norm(x, g, eps=RMS_EPS):
    xf = x.astype(jnp.float32)
    y = xf * lax.rsqrt(jnp.mean(xf * xf, axis=-1, keepdims=True) + eps)
    return (y * g.astype(jnp.float32)).astype(x.dtype)


def layer_norm(x, g, b, eps=LN_EPS):
    xf = x.astype(jnp.float32)
    xc = xf - jnp.mean(xf, axis=-1, keepdims=True)
    y = xc * lax.rsqrt(jnp.mean(xc * xc, axis=-1, keepdims=True) + eps)
    return (y * g.astype(jnp.float32) + b.astype(jnp.float32)).astype(x.dtype)


def modulate(h, shift, scale):
    return h * (1.0 + scale) + shift


def depthwise_conv(x, w):
    k = w.shape[0]
    return lax.conv_general_dilated(
        x, w.astype(x.dtype)[:, None, :], window_strides=(1,),
        padding=[(k // 2, k // 2)], dimension_numbers=('NWC', 'WIO', 'NWC'),
        feature_group_count=x.shape[-1])


def short_conv_mixer(h, w_in, conv_w, w_out):
    b_gate, c_gate, u = jnp.split(h @ w_in, 3, axis=-1)
    return (b_gate * depthwise_conv(c_gate * u, conv_w)) @ w_out


def conformer_conv(h, w_pw1, b_pw1, dw_w, dw_b, ln_g, ln_b, w_pw2, b_pw2):
    a, g = jnp.split(h @ w_pw1 + b_pw1, 2, axis=-1)
    u = a * jax.nn.sigmoid(g)
    u = depthwise_conv(u, dw_w) + dw_b
    u = jax.nn.silu(layer_norm(u, ln_g, ln_b))
    return u @ w_pw2 + b_pw2


def axial_rope_tables(n_tokens):
    rows = n_tokens // GRID_W
    row = jnp.repeat(jnp.arange(rows, dtype=jnp.float32), GRID_W)
    col = jnp.tile(jnp.arange(GRID_W, dtype=jnp.float32), rows)
    inv_freq = ROPE_BASE ** (-jnp.arange(ROPE_FREQS, dtype=jnp.float32) / ROPE_FREQS)
    ang = jnp.stack([row[:, None] * inv_freq, col[:, None] * inv_freq], axis=1)
    return jnp.cos(ang), jnp.sin(ang)


def apply_axial_rope(t, cos, sin):
    tr = t.reshape(*t.shape[:-1], 2, 2, ROPE_FREQS).astype(jnp.float32)
    a, b = tr[..., 0, :], tr[..., 1, :]
    cs, sn = cos[None, :, None, None], sin[None, :, None, None]
    out = jnp.stack([a * cs - b * sn, a * sn + b * cs], axis=-2)
    return out.reshape(t.shape).astype(t.dtype)


def diff_lambda_init(layer):
    return 0.8 - 0.6 * math.exp(-0.3 * layer)


def diff_softmax_attend(q1, q2, k1, k2, v, lam):
    scale = DA_HEAD_DIM ** -0.5
    s1 = jnp.einsum('bhqd,bhkd->bhqk', q1, k1, preferred_element_type=jnp.float32) * scale
    s2 = jnp.einsum('bhqd,bhkd->bhqk', q2, k2, preferred_element_type=jnp.float32) * scale
    p = jax.nn.softmax(s1, axis=-1) - lam * jax.nn.softmax(s2, axis=-1)
    return jnp.einsum('bhqk,bhkd->bhqd', p.astype(v.dtype), v)


def diff_attention(a_lat, a_ctx, w_qkv, q_norm_g, k_norm_g, lam_q1, lam_k1, lam_q2, lam_k2,
                   subln_g, w_o, lambda_init, with_ctx_out):
    bsz, n_lat, d = a_lat.shape
    f32 = jnp.float32
    lam = (jnp.exp(jnp.sum(lam_q1.astype(f32) * lam_k1.astype(f32)))
           - jnp.exp(jnp.sum(lam_q2.astype(f32) * lam_k2.astype(f32))) + lambda_init)

    def project(h):
        b, n, _ = h.shape
        q, k, v = jnp.split(h @ w_qkv, 3, axis=-1)
        q = rms_norm(q.reshape(b, n, DA_HEADS, 2, DA_HEAD_DIM), q_norm_g)
        k = rms_norm(k.reshape(b, n, DA_HEADS, 2, DA_HEAD_DIM), k_norm_g)
        return q, k, v.reshape(b, n, DA_HEADS, DA_V_DIM).transpose(0, 2, 1, 3)

    def heads(t, comp):
        return t[:, :, :, comp].transpose(0, 2, 1, 3)

    def finish(o):
        o = rms_norm(o, subln_g) * (1.0 - lambda_init)
        return o.reshape(o.shape[0], o.shape[1], d) @ w_o

    q_l, k_l, v_l = project(a_lat)
    cos, sin = axial_rope_tables(n_lat)
    q_l = apply_axial_rope(q_l, cos, sin)
    k_l = apply_axial_rope(k_l, cos, sin)
    q_c, k_c, v_c = project(a_ctx)
    k1 = jnp.concatenate([heads(k_l, 0), heads(k_c, 0)], axis=2)
    k2 = jnp.concatenate([heads(k_l, 1), heads(k_c, 1)], axis=2)
    v = jnp.concatenate([v_l, v_c], axis=2)

    n_blk = n_lat // Q_BLOCK

    def blocks(t):
        return t.reshape(bsz, DA_HEADS, n_blk, Q_BLOCK, DA_HEAD_DIM).transpose(2, 0, 1, 3, 4)

    o = lax.map(lambda qb: diff_softmax_attend(qb[0], qb[1], k1, k2, v, lam),
                (blocks(heads(q_l, 0)), blocks(heads(q_l, 1))))
    o_lat = o.transpose(1, 0, 3, 2, 4).reshape(bsz, n_lat, DA_HEADS, DA_V_DIM)
    y_lat = finish(o_lat)
    if not with_ctx_out:
        return y_lat, None
    o_ctx = diff_softmax_attend(heads(q_c, 0), heads(q_c, 1), heads(k_c, 0), heads(k_c, 1), v_c, lam)
    return y_lat, finish(o_ctx.transpose(0, 2, 1, 3))


def peer(h, w_query, sub_keys, expert_u, expert_v):
    n_tok, d = h.shape
    q = (h @ w_query).reshape(n_tok, PEER_HEADS, 2, PEER_HALF)
    s = jnp.einsum('thpd,hpnd->thpn', q, sub_keys, preferred_element_type=jnp.float32)
    v_half, i_half = lax.top_k(s, PEER_TOPK)
    cand_s = (v_half[:, :, 0, :, None] + v_half[:, :, 1, None, :]).reshape(n_tok, PEER_HEADS, -1)
    cand_i = (i_half[:, :, 0, :, None] * PEER_KEYS + i_half[:, :, 1, None, :]).reshape(n_tok, PEER_HEADS, -1)
    top_s, top_pos = lax.top_k(cand_s, PEER_TOPK)
    idx = jnp.take_along_axis(cand_i, top_pos, axis=-1)
    gates = jax.nn.softmax(top_s, axis=-1)
    n_chunk = n_tok // PEER_CHUNK
    hk = PEER_HEADS * PEER_TOPK

    def experts(args):
        hb, ib, gb = args
        u = jnp.take(expert_u, ib, axis=0)
        z = jnp.einsum('td,tkd->tk', hb, u, preferred_element_type=jnp.float32)
        act = jax.nn.gelu(z, approximate=False) * gb
        return jnp.einsum('tk,tkd->td', act.astype(hb.dtype), jnp.take(expert_v, ib, axis=0))

    out = lax.map(experts, (h.reshape(n_chunk, PEER_CHUNK, d),
                            idx.reshape(n_chunk, PEER_CHUNK, hk),
                            gates.reshape(n_chunk, PEER_CHUNK, hk)))
    return out.reshape(n_tok, d)


def setup_inputs(seed: int = 0) -> dict:
    key = jax.random.key(seed)
    ks = iter(jax.random.split(key, 40))
    D = D_MODEL

    def nrm(shape, std):
        return jax.random.normal(next(ks), shape, jnp.float32) * std

    return {
        'x': nrm((BATCH, SEQ, D), 1.0),
        'c': nrm((BATCH, D), 1.0),
        'ctx': nrm((BATCH, CTX_LEN, D), 1.0),
        'c_ctx': nrm((D,), 1.0),
        'w_mod': nrm((DEPTH, D, 6 * D), 0.5 * D ** -0.5),
        'b_mod': nrm((DEPTH, 6 * D), 0.02),
        'norm1_g': 1.0 + nrm((DEPTH, D), 0.02),
        'norm2_g': 1.0 + nrm((DEPTH, D), 0.02),
        'sc_w_in': nrm((N_A, D, 3 * D), D ** -0.5),
        'sc_conv_w': nrm((N_A, SC_WIDTH, D), SC_WIDTH ** -0.5),
        'sc_w_out': nrm((N_A, D, D), D ** -0.5),
        'da_w_qkv': nrm((N_B, D, 3 * D), D ** -0.5),
        'da_q_norm_g': 1.0 + nrm((N_B, DA_HEAD_DIM), 0.02),
        'da_k_norm_g': 1.0 + nrm((N_B, DA_HEAD_DIM), 0.02),
        'da_lam_q1': nrm((N_B, DA_HEAD_DIM), 0.1),
        'da_lam_k1': nrm((N_B, DA_HEAD_DIM), 0.1),
        'da_lam_q2': nrm((N_B, DA_HEAD_DIM), 0.1),
        'da_lam_k2': nrm((N_B, DA_HEAD_DIM), 0.1),
        'da_subln_g': 1.0 + nrm((N_B, DA_V_DIM), 0.02),
        'da_w_o': nrm((N_B, D, D), D ** -0.5),
        'cf_w_pw1': nrm((N_C, D, 2 * D), D ** -0.5),
        'cf_b_pw1': nrm((N_C, 2 * D), 0.02),
        'cf_dw_w': nrm((N_C, CF_WIDTH, D), CF_WIDTH ** -0.5),
        'cf_dw_b': nrm((N_C, D), 0.02),
        'cf_ln_g': 1.0 + nrm((N_C, D), 0.02),
        'cf_ln_b': nrm((N_C, D), 0.02),
        'cf_w_pw2': nrm((N_C, D, D), D ** -0.5),
        'cf_b_pw2': nrm((N_C, D), 0.02),
        'peer_w_query': nrm((DEPTH, D, PEER_HEADS * PEER_QDIM), D ** -0.5),
        'peer_sub_keys': nrm((DEPTH, PEER_HEADS, 2, PEER_KEYS, PEER_HALF), PEER_HALF ** -0.5),
        'peer_u': nrm((DEPTH, PEER_EXPERTS, D), D ** -0.5),
        'peer_v': nrm((DEPTH, PEER_EXPERTS, D), PEER_HEADS ** -0.5),
    }


def reference(x, c, ctx, c_ctx, w_mod, b_mod, norm1_g, norm2_g,
              sc_w_in, sc_conv_w, sc_w_out,
              da_w_qkv, da_q_norm_g, da_k_norm_g, da_lam_q1, da_lam_k1,
              da_lam_q2, da_lam_k2, da_subln_g, da_w_o,
              cf_w_pw1, cf_b_pw1, cf_dw_w, cf_dw_b, cf_ln_g, cf_ln_b, cf_w_pw2, cf_b_pw2,
              peer_w_query, peer_sub_keys, peer_u, peer_v):
    bsz, n_lat, d = x.shape
    n_ctx = ctx.shape[1]
    ctx_read = [i for i in range(DEPTH) if i % N_MIXERS == 1]
    last_ctx_read = ctx_read[-1] if ctx_read else -1
    s_lat = jax.nn.silu(c)
    s_ctx = jax.nn.silu(c_ctx)[None]
    h_lat, h_ctx = x, ctx
    for i in range(DEPTH):
        kind, j = i % N_MIXERS, i // N_MIXERS
        use_ctx = i <= last_ctx_read
        adv_ctx = i < last_ctx_read
        m_lat = jnp.split((s_lat @ w_mod[i] + b_mod[i])[:, None, :], 6, axis=-1)
        a_lat = modulate(rms_norm(h_lat, norm1_g[i]), m_lat[0], m_lat[1])
        if use_ctx:
            m_ctx = jnp.split((s_ctx @ w_mod[i] + b_mod[i])[:, None, :], 6, axis=-1)
            a_ctx = modulate(rms_norm(h_ctx, norm1_g[i]), m_ctx[0], m_ctx[1])
        if kind == 0:
            y_lat = short_conv_mixer(a_lat, sc_w_in[j], sc_conv_w[j], sc_w_out[j])
            if adv_ctx:
                y_ctx = short_conv_mixer(a_ctx, sc_w_in[j], sc_conv_w[j], sc_w_out[j])
        elif kind == 1:
            y_lat, y_ctx = diff_attention(a_lat, a_ctx, da_w_qkv[j], da_q_norm_g[j], da_k_norm_g[j],
                                          da_lam_q1[j], da_lam_k1[j], da_lam_q2[j], da_lam_k2[j],
                                          da_subln_g[j], da_w_o[j], diff_lambda_init(i), adv_ctx)
        else:
            y_lat = conformer_conv(a_lat, cf_w_pw1[j], cf_b_pw1[j], cf_dw_w[j], cf_dw_b[j],
                                   cf_ln_g[j], cf_ln_b[j], cf_w_pw2[j], cf_b_pw2[j])
            if adv_ctx:
                y_ctx = conformer_conv(a_ctx, cf_w_pw1[j], cf_b_pw1[j], cf_dw_w[j], cf_dw_b[j],
                                       cf_ln_g[j], cf_ln_b[j], cf_w_pw2[j], cf_b_pw2[j])
        h_lat = h_lat + m_lat[2] * y_lat
        f_lat = modulate(rms_norm(h_lat, norm2_g[i]), m_lat[3], m_lat[4]).reshape(-1, d)
        if adv_ctx:
            h_ctx = h_ctx + m_ctx[2] * y_ctx
            f_ctx = modulate(rms_norm(h_ctx, norm2_g[i]), m_ctx[3], m_ctx[4]).reshape(-1, d)
            f_all = jnp.concatenate([f_lat, f_ctx], axis=0)
        else:
            f_all = f_lat
        y_all = peer(f_all, peer_w_query[i], peer_sub_keys[i], peer_u[i], peer_v[i])
        h_lat = h_lat + m_lat[5] * y_all[:bsz * n_lat].reshape(bsz, n_lat, d)
        if adv_ctx:
            h_ctx = h_ctx + m_ctx[5] * y_all[bsz * n_lat:].reshape(bsz, n_ctx, d)
    return h_lat
```

```python
import functools
import math

import jax
import jax.numpy as jnp
from jax import lax
from jax.experimental import pallas as pl
from jax.experimental.pallas import tpu as pltpu

D_MODEL = 1024
DEPTH = 4
GRID_W = 64
N_MIXERS = 3
DA_HEADS = 8
DA_HEAD_DIM = D_MODEL // (2 * DA_HEADS)
DA_V_DIM = 2 * DA_HEAD_DIM
ROPE_FREQS = DA_HEAD_DIM // 4
ROPE_BASE = 10000.0
Q_BLOCK = 128
PEER_HEADS = 8
PEER_KEYS = 128
PEER_EXPERTS = PEER_KEYS * PEER_KEYS
PEER_QDIM = 256
PEER_HALF = PEER_QDIM // 2
PEER_TOPK = 16
PEER_HK = PEER_HEADS * PEER_TOPK
RMS_EPS = 1e-6
LN_EPS = 1e-5

LANES = 128
HALF_D = D_MODEL // 2
ROW_SUBLANES = HALF_D // LANES
PEER_TOKEN_BLOCK = 128
PEER_VMEM_LIMIT = 50 * 1024 * 1024


def _pack_table(tbl):
    bits = lax.bitcast_convert_type(tbl.astype(jnp.bfloat16), jnp.uint16).astype(jnp.uint32)
    words = bits[:, :HALF_D] | (bits[:, HALF_D:] << 16)
    return lax.bitcast_convert_type(words, jnp.int32).reshape(tbl.shape[0] * ROW_SUBLANES, LANES)


def _table_row(tbl_ref, first_sublane):
    return tbl_ref[pl.ds(pl.multiple_of(first_sublane, ROW_SUBLANES), ROW_SUBLANES), :]


def _unpack(w):
    lo = lax.bitcast_convert_type(w << 16, jnp.float32)
    hi = lax.bitcast_convert_type(w & jnp.int32(-65536), jnp.float32)
    return lo, hi


_P_STRIDE = PEER_HK + 1


def _peer_z_kernel(idx_ref, h_ref, tbl_ref, z_ref, p_a, p_b):
    tb = z_ref.shape[0]
    ones = jnp.ones((8, 2 * LANES), jnp.bfloat16)

    def gather(t, p_scr):
        hl = h_ref[t, 0]
        hh = h_ref[t, 1]
        rows = idx_ref.at[t]
        for k in range(PEER_HK):
            lo, hi = _unpack(_table_row(tbl_ref, rows[k]))
            p_scr[pl.ds(k, ROW_SUBLANES, stride=_P_STRIDE), :] = lo * hl + hi * hh

    def reduce(t, p_scr):
        ps = p_scr[pl.ds(0, PEER_HK), :]
        for s in range(1, ROW_SUBLANES):
            ps = ps + p_scr[pl.ds(s * _P_STRIDE, PEER_HK), :]
        ps_hi = ps.astype(jnp.bfloat16)
        ps_lo = (ps - ps_hi.astype(jnp.float32)).astype(jnp.bfloat16)
        z = lax.dot_general(ones, jnp.concatenate([ps_hi, ps_lo], axis=1), (((1,), (1,)), ((), ())),
                            preferred_element_type=jnp.float32)
        z_ref[pl.ds(t, 1), :] = z[0:1]

    gather(0, p_a)

    def token_pair(i, carry):
        t = 2 * i
        reduce(t, p_a)
        gather(t + 1, p_b)
        reduce(t + 1, p_b)
        gather(jnp.minimum(t + 2, tb - 1), p_a)
        return carry

    lax.fori_loop(0, tb // 2, token_pair, 0)


def _peer_out_kernel(idx_ref, act_ref, tbl_ref, o_ref, b_a, b_b):
    tb = o_ref.shape[0]
    n_acc = 2

    def spread(t, b_scr):
        b_scr[...] = jnp.broadcast_to(act_ref[pl.ds(t, 1), :], (PEER_HK, LANES)).T

    def accumulate(t, b_scr):
        acc_lo = [jnp.zeros((ROW_SUBLANES, LANES), jnp.float32) for _ in range(n_acc)]
        acc_hi = [jnp.zeros((ROW_SUBLANES, LANES), jnp.float32) for _ in range(n_acc)]
        rows = idx_ref.at[t]
        for k in range(PEER_HK):
            lo, hi = _unpack(_table_row(tbl_ref, rows[k]))
            a = jnp.broadcast_to(b_scr[pl.ds(k, 1), :], (ROW_SUBLANES, LANES))
            acc_lo[k % n_acc] = acc_lo[k % n_acc] + a * lo
            acc_hi[k % n_acc] = acc_hi[k % n_acc] + a * hi
        o_ref[t, 0] = acc_lo[0] + acc_lo[1]
        o_ref[t, 1] = acc_hi[0] + acc_hi[1]

    spread(0, b_a)

    def token_pair(i, carry):
        t = 2 * i
        spread(t + 1, b_b)
        accumulate(t, b_a)
        spread(jnp.minimum(t + 2, tb - 1), b_a)
        accumulate(t + 1, b_b)
        return carry

    lax.fori_loop(0, tb // 2, token_pair, 0)


def _table_spec(n_rows):
    return pl.BlockSpec((n_rows, LANES), lambda i: (0, 0), pipeline_mode=pl.Buffered(1))


def _peer_experts(h, idx, gates, tbl_u, tbl_v):
    n_tok = h.shape[0]
    n_exp = tbl_u.shape[0]
    idx = idx * ROW_SUBLANES
    tb = PEER_TOKEN_BLOCK
    grid = (n_tok // tb,)
    smem_spec = pl.BlockSpec((tb, PEER_HK), lambda i: (i, 0), memory_space=pltpu.SMEM)
    row_spec = pl.BlockSpec((tb, 2, ROW_SUBLANES, LANES), lambda i: (i, 0, 0, 0))
    params = pltpu.CompilerParams(dimension_semantics=("arbitrary",), vmem_limit_bytes=PEER_VMEM_LIMIT)
    z = pl.pallas_call(
        _peer_z_kernel,
        grid=grid,
        in_specs=[smem_spec, row_spec, _table_spec(n_exp)],
        out_specs=pl.BlockSpec((tb, PEER_HK), lambda i: (i, 0)),
        out_shape=jax.ShapeDtypeStruct((n_tok, PEER_HK), jnp.float32),
        scratch_shapes=[pltpu.VMEM((ROW_SUBLANES * _P_STRIDE, LANES), jnp.float32)] * 2,
        compiler_params=params,
        name="peer_z",
    )(idx, h.reshape(n_tok, 2, ROW_SUBLANES, LANES), tbl_u)
    act = jax.nn.gelu(z, approximate=False) * gates
    out = pl.pallas_call(
        _peer_out_kernel,
        grid=grid,
        in_specs=[smem_spec, pl.BlockSpec((tb, PEER_HK), lambda i: (i, 0)), _table_spec(n_exp)],
        out_specs=row_spec,
        out_shape=jax.ShapeDtypeStruct((n_tok, 2, ROW_SUBLANES, LANES), jnp.float32),
        scratch_shapes=[pltpu.VMEM((PEER_HK, LANES), jnp.float32)] * 2,
        compiler_params=params,
        name="peer_out",
    )(idx, act, tbl_v)
    return out.reshape(n_tok, D_MODEL)


def _peer(h, w_query, sub_keys, expert_u, expert_v):
    n_tok = h.shape[0]
    q = (h @ w_query).reshape(n_tok, PEER_HEADS, 2, PEER_HALF)
    s = jnp.einsum('thpd,hpnd->thpn', q, sub_keys, preferred_element_type=jnp.float32)
    v_half, i_half = lax.top_k(s, PEER_TOPK)
    cand_s = (v_half[:, :, 0, :, None] + v_half[:, :, 1, None, :]).reshape(n_tok, PEER_HEADS, -1)
    cand_i = (i_half[:, :, 0, :, None] * PEER_KEYS + i_half[:, :, 1, None, :]).reshape(n_tok, PEER_HEADS, -1)
    top_s, top_pos = lax.top_k(cand_s, PEER_TOPK)
    idx = jnp.take_along_axis(cand_i, top_pos, axis=-1)
    gates = jax.nn.softmax(top_s, axis=-1)
    return _peer_experts(h, idx.reshape(n_tok, PEER_HK).astype(jnp.int32), gates.reshape(n_tok, PEER_HK),
                         _pack_table(expert_u), _pack_table(expert_v))


def _rms_norm(x, g, eps=RMS_EPS):
    y = x * lax.rsqrt(jnp.mean(x * x, axis=-1, keepdims=True) + eps)
    return y * g


def _layer_norm(x, g, b, eps=LN_EPS):
    xc = x - jnp.mean(x, axis=-1, keepdims=True)
    y = xc * lax.rsqrt(jnp.mean(xc * xc, axis=-1, keepdims=True) + eps)
    return y * g + b


def _modulate(h, shift, scale):
    return h * (1.0 + scale) + shift


def _depthwise_conv(x, w):
    k = w.shape[0]
    return lax.conv_general_dilated(
        x, w[:, None, :], window_strides=(1,), padding=[(k // 2, k // 2)],
        dimension_numbers=('NWC', 'WIO', 'NWC'), feature_group_count=x.shape[-1])


def _short_conv_mixer(h, w_in, conv_w, w_out):
    b_gate, c_gate, u = jnp.split(h @ w_in, 3, axis=-1)
    return (b_gate * _depthwise_conv(c_gate * u, conv_w)) @ w_out


def _conformer_conv(h, w_pw1, b_pw1, dw_w, dw_b, ln_g, ln_b, w_pw2, b_pw2):
    a, g = jnp.split(h @ w_pw1 + b_pw1, 2, axis=-1)
    u = a * jax.nn.sigmoid(g)
    u = _depthwise_conv(u, dw_w) + dw_b
    u = jax.nn.silu(_layer_norm(u, ln_g, ln_b))
    return u @ w_pw2 + b_pw2


def _axial_rope_tables(n_tokens):
    rows = n_tokens // GRID_W
    row = jnp.repeat(jnp.arange(rows, dtype=jnp.float32), GRID_W)
    col = jnp.tile(jnp.arange(GRID_W, dtype=jnp.float32), rows)
    inv_freq = ROPE_BASE ** (-jnp.arange(ROPE_FREQS, dtype=jnp.float32) / ROPE_FREQS)
    ang = jnp.stack([row[:, None] * inv_freq, col[:, None] * inv_freq], axis=1)
    return jnp.cos(ang), jnp.sin(ang)


def _apply_axial_rope(t, cos, sin):
    tr = t.reshape(*t.shape[:-1], 2, 2, ROPE_FREQS)
    a, b = tr[..., 0, :], tr[..., 1, :]
    cs, sn = cos[None, :, None, None], sin[None, :, None, None]
    out = jnp.stack([a * cs - b * sn, a * sn + b * cs], axis=-2)
    return out.reshape(t.shape)


def _diff_softmax_attend(q1, q2, k1, k2, v, lam):
    scale = DA_HEAD_DIM ** -0.5
    s1 = jnp.einsum('bhqd,bhkd->bhqk', q1, k1, preferred_element_type=jnp.float32) * scale
    s2 = jnp.einsum('bhqd,bhkd->bhqk', q2, k2, preferred_element_type=jnp.float32) * scale
    p = jax.nn.softmax(s1, axis=-1) - lam * jax.nn.softmax(s2, axis=-1)
    return jnp.einsum('bhqk,bhkd->bhqd', p, v)


def _diff_attention(a_lat, a_ctx, w_qkv, q_norm_g, k_norm_g, lam_q1, lam_k1, lam_q2, lam_k2,
                    subln_g, w_o, lambda_init, with_ctx_out):
    bsz, n_lat, d = a_lat.shape
    lam = (jnp.exp(jnp.sum(lam_q1 * lam_k1)) - jnp.exp(jnp.sum(lam_q2 * lam_k2)) + lambda_init)

    def project(h):
        b, n, _ = h.shape
        q, k, v = jnp.split(h @ w_qkv, 3, axis=-1)
        q = _rms_norm(q.reshape(b, n, DA_HEADS, 2, DA_HEAD_DIM), q_norm_g)
        k = _rms_norm(k.reshape(b, n, DA_HEADS, 2, DA_HEAD_DIM), k_norm_g)
        return q, k, v.reshape(b, n, DA_HEADS, DA_V_DIM).transpose(0, 2, 1, 3)

    def heads(t, comp):
        return t[:, :, :, comp].transpose(0, 2, 1, 3)

    def finish(o):
        o = _rms_norm(o, subln_g) * (1.0 - lambda_init)
        return o.reshape(o.shape[0], o.shape[1], d) @ w_o

    q_l, k_l, v_l = project(a_lat)
    cos, sin = _axial_rope_tables(n_lat)
    q_l = _apply_axial_rope(q_l, cos, sin)
    k_l = _apply_axial_rope(k_l, cos, sin)
    q_c, k_c, v_c = project(a_ctx)
    k1 = jnp.concatenate([heads(k_l, 0), heads(k_c, 0)], axis=2)
    k2 = jnp.concatenate([heads(k_l, 1), heads(k_c, 1)], axis=2)
    v = jnp.concatenate([v_l, v_c], axis=2)
    n_blk = n_lat // Q_BLOCK

    def blocks(t):
        return t.reshape(bsz, DA_HEADS, n_blk, Q_BLOCK, DA_HEAD_DIM).transpose(2, 0, 1, 3, 4)

    o = lax.map(lambda qb: _diff_softmax_attend(qb[0], qb[1], k1, k2, v, lam),
                (blocks(heads(q_l, 0)), blocks(heads(q_l, 1))))
    o_lat = o.transpose(1, 0, 3, 2, 4).reshape(bsz, n_lat, DA_HEADS, DA_V_DIM)
    y_lat = finish(o_lat)
    if not with_ctx_out:
        return y_lat, None
    o_ctx = _diff_softmax_attend(heads(q_c, 0), heads(q_c, 1), heads(k_c, 0), heads(k_c, 1), v_c, lam)
    return y_lat, finish(o_ctx.transpose(0, 2, 1, 3))


def kernel(x, c, ctx, c_ctx, w_mod, b_mod, norm1_g, norm2_g, sc_w_in, sc_conv_w, sc_w_out, da_w_qkv, da_q_norm_g, da_k_norm_g, da_lam_q1, da_lam_k1, da_lam_q2, da_lam_k2, da_subln_g, da_w_o, cf_w_pw1, cf_b_pw1, cf_dw_w, cf_dw_b, cf_ln_g, cf_ln_b, cf_w_pw2, cf_b_pw2, peer_w_query, peer_sub_keys, peer_u, peer_v):
    bsz, n_lat, d = x.shape
    n_ctx = ctx.shape[1]
    ctx_read = [i for i in range(DEPTH) if i % N_MIXERS == 1]
    last_ctx_read = ctx_read[-1] if ctx_read else -1
    s_lat = jax.nn.silu(c)
    s_ctx = jax.nn.silu(c_ctx)[None]
    h_lat, h_ctx = x, ctx
    for i in range(DEPTH):
        kind, j = i % N_MIXERS, i // N_MIXERS
        use_ctx = i <= last_ctx_read
        adv_ctx = i < last_ctx_read
        m_lat = jnp.split((s_lat @ w_mod[i] + b_mod[i])[:, None, :], 6, axis=-1)
        a_lat = _modulate(_rms_norm(h_lat, norm1_g[i]), m_lat[0], m_lat[1])
        if use_ctx:
            m_ctx = jnp.split((s_ctx @ w_mod[i] + b_mod[i])[:, None, :], 6, axis=-1)
            a_ctx = _modulate(_rms_norm(h_ctx, norm1_g[i]), m_ctx[0], m_ctx[1])
        if kind == 0:
            y_lat = _short_conv_mixer(a_lat, sc_w_in[j], sc_conv_w[j], sc_w_out[j])
            if adv_ctx:
                y_ctx = _short_conv_mixer(a_ctx, sc_w_in[j], sc_conv_w[j], sc_w_out[j])
        elif kind == 1:
            y_lat, y_ctx = _diff_attention(a_lat, a_ctx, da_w_qkv[j], da_q_norm_g[j], da_k_norm_g[j],
                                           da_lam_q1[j], da_lam_k1[j], da_lam_q2[j], da_lam_k2[j],
                                           da_subln_g[j], da_w_o[j], 0.8 - 0.6 * math.exp(-0.3 * i), adv_ctx)
        else:
            y_lat = _conformer_conv(a_lat, cf_w_pw1[j], cf_b_pw1[j], cf_dw_w[j], cf_dw_b[j],
                                    cf_ln_g[j], cf_ln_b[j], cf_w_pw2[j], cf_b_pw2[j])
            if adv_ctx:
                y_ctx = _conformer_conv(a_ctx, cf_w_pw1[j], cf_b_pw1[j], cf_dw_w[j], cf_dw_b[j],
                                        cf_ln_g[j], cf_ln_b[j], cf_w_pw2[j], cf_b_pw2[j])
        h_lat = h_lat + m_lat[2] * y_lat
        f_lat = _modulate(_rms_norm(h_lat, norm2_g[i]), m_lat[3], m_lat[4]).reshape(-1, d)
        if adv_ctx:
            h_ctx = h_ctx + m_ctx[2] * y_ctx
            f_ctx = _modulate(_rms_norm(h_ctx, norm2_g[i]), m_ctx[3], m_ctx[4]).reshape(-1, d)
            f_all = jnp.concatenate([f_lat, f_ctx], axis=0)
        else:
            f_all = f_lat
        y_all = _peer(f_all, peer_w_query[i], peer_sub_keys[i], peer_u[i], peer_v[i])
        h_lat = h_lat + m_lat[5] * y_all[:bsz * n_lat].reshape(bsz, n_lat, d)
        if adv_ctx:
            h_ctx = h_ctx + m_ctx[5] * y_all[bsz * n_lat:].reshape(bsz, n_ctx, d)
    return h_lat
```

```python
import functools
import math

import jax
import jax.numpy as jnp
from jax import lax
from jax.experimental import pallas as pl
from jax.experimental.pallas import tpu as pltpu

D_MODEL = 1024
DEPTH = 4
GRID_W = 64
N_MIXERS = 3
DA_HEADS = 8
DA_HEAD_DIM = D_MODEL // (2 * DA_HEADS)
DA_V_DIM = 2 * DA_HEAD_DIM
ROPE_FREQS = DA_HEAD_DIM // 4
ROPE_BASE = 10000.0
Q_BLOCK = 128
PEER_HEADS = 8
PEER_KEYS = 128
PEER_EXPERTS = PEER_KEYS * PEER_KEYS
PEER_QDIM = 256
PEER_HALF = PEER_QDIM // 2
PEER_TOPK = 16
PEER_HK = PEER_HEADS * PEER_TOPK
RMS_EPS = 1e-6
LN_EPS = 1e-5

LANES = 128
HALF_D = D_MODEL // 2
ROW_SUBLANES = HALF_D // LANES
PEER_TOKEN_BLOCK = 128
PEER_VMEM_LIMIT = 50 * 1024 * 1024


def _pack_table(tbl):
    bits = lax.bitcast_convert_type(tbl.astype(jnp.bfloat16), jnp.uint16).astype(jnp.uint32)
    words = bits[:, :HALF_D] | (bits[:, HALF_D:] << 16)
    return lax.bitcast_convert_type(words, jnp.int32).reshape(tbl.shape[0] * ROW_SUBLANES, LANES)


def _table_row(tbl_ref, first_sublane):
    return tbl_ref[pl.ds(pl.multiple_of(first_sublane, ROW_SUBLANES), ROW_SUBLANES), :]


def _unpack(w):
    lo = lax.bitcast_convert_type(w << 16, jnp.float32)
    hi = lax.bitcast_convert_type(w & jnp.int32(-65536), jnp.float32)
    return lo, hi


_P_STRIDE = PEER_HK + 1


def _peer_z_kernel(idx_ref, h_ref, tbl_ref, z_ref, p_a, p_b):
    tb = z_ref.shape[0]
    ones = jnp.ones((8, 2 * LANES), jnp.bfloat16)

    def gather(t, p_scr):
        hl = h_ref[t, 0]
        hh = h_ref[t, 1]
        rows = idx_ref.at[t]
        for k in range(PEER_HK):
            lo, hi = _unpack(_table_row(tbl_ref, rows[k]))
            p_scr[pl.ds(k, ROW_SUBLANES, stride=_P_STRIDE), :] = lo * hl + hi * hh

    def reduce(t, p_scr):
        ps = p_scr[pl.ds(0, PEER_HK), :]
        for s in range(1, ROW_SUBLANES):
            ps = ps + p_scr[pl.ds(s * _P_STRIDE, PEER_HK), :]
        ps_hi = ps.astype(jnp.bfloat16)
        ps_lo = (ps - ps_hi.astype(jnp.float32)).astype(jnp.bfloat16)
        z = lax.dot_general(ones, jnp.concatenate([ps_hi, ps_lo], axis=1), (((1,), (1,)), ((), ())),
                            preferred_element_type=jnp.float32)
        z_ref[pl.ds(t, 1), :] = z[0:1]

    gather(0, p_a)

    def token_pair(i, carry):
        t = 2 * i
        reduce(t, p_a)
        gather(t + 1, p_b)
        reduce(t + 1, p_b)
        gather(jnp.minimum(t + 2, tb - 1), p_a)
        return carry

    lax.fori_loop(0, tb // 2, token_pair, 0)


def _peer_out_kernel(idx_ref, z_ref, gate_ref, tbl_ref, o_ref, act_ref, b_a, b_b):
    tb = o_ref.shape[0]
    n_acc = 2
    z = z_ref[...]
    act_ref[...] = 0.5 * z * (1.0 + lax.erf(z * (2.0 ** -0.5))) * gate_ref[...]

    def spread(t, b_scr):
        b_scr[...] = jnp.broadcast_to(act_ref[pl.ds(t, 1), :], (PEER_HK, LANES)).T

    def accumulate(t, b_scr):
        acc_lo = [jnp.zeros((ROW_SUBLANES, LANES), jnp.float32) for _ in range(n_acc)]
        acc_hi = [jnp.zeros((ROW_SUBLANES, LANES), jnp.float32) for _ in range(n_acc)]
        rows = idx_ref.at[t]
        for k in range(PEER_HK):
            lo, hi = _unpack(_table_row(tbl_ref, rows[k]))
            a = jnp.broadcast_to(b_scr[pl.ds(k, 1), :], (ROW_SUBLANES, LANES))
            acc_lo[k % n_acc] = acc_lo[k % n_acc] + a * lo
            acc_hi[k % n_acc] = acc_hi[k % n_acc] + a * hi
        o_ref[t, 0] = acc_lo[0] + acc_lo[1]
        o_ref[t, 1] = acc_hi[0] + acc_hi[1]

    spread(0, b_a)

    def token_pair(i, carry):
        t = 2 * i
        spread(t + 1, b_b)
        accumulate(t, b_a)
        spread(jnp.minimum(t + 2, tb - 1), b_a)
        accumulate(t + 1, b_b)
        return carry

    lax.fori_loop(0, tb // 2, token_pair, 0)


def _table_spec(n_rows):
    return pl.BlockSpec((n_rows, LANES), lambda i: (0, 0), pipeline_mode=pl.Buffered(1))


def _split_bf16(x):
    hi = x.astype(jnp.bfloat16)
    return hi, (x - hi.astype(jnp.float32)).astype(jnp.bfloat16)


def _dot3(a_hi, a_lo, b_hi, b_lo, dims):
    dot = functools.partial(lax.dot_general, dimension_numbers=dims, preferred_element_type=jnp.float32)
    return dot(a_hi, b_hi) + (dot(a_lo, b_hi) + dot(a_hi, b_lo))


_NN = (((1,), (0,)), ((), ()))
_NT = (((1,), (1,)), ((), ()))
ROUTE_BLOCK = 512
ROUTE_SUB = LANES
_NEG = -jnp.inf


def _extract_top(scores, tag, n_top, out_refs, payloads=None):
    scores = list(scores)
    for r in range(n_top):
        for i, score in enumerate(scores):
            m = jnp.max(score, axis=0, keepdims=True)
            at = jnp.min(jnp.where(score == m, tag, 1e9), axis=0, keepdims=True)
            sel = tag == at
            out_refs[i][0][r:r + 1, :] = m
            if payloads is None:
                out_refs[i][1][r:r + 1, :] = at
            else:
                out_refs[i][1][r:r + 1, :] = jnp.max(jnp.where(sel, payloads[i], -1.0), axis=0, keepdims=True)
            scores[i] = jnp.where(sel, _NEG, score)


def _peer_route_kernel(f_ref, wq_hi_ref, wq_lo_ref, key_hi_ref, key_lo_ref, idx_ref, gate_ref,
                       q_scr, v_scr, i_scr, top_scr, e_scr, idx_t, gate_t):
    rb = f_ref.shape[0]
    f_hi, f_lo = _split_bf16(f_ref[...])
    for h in range(PEER_HEADS):
        cols = slice(h * PEER_QDIM, (h + 1) * PEER_QDIM)
        q = _dot3(f_hi, f_lo, wq_hi_ref[:, cols], wq_lo_ref[:, cols], _NN)
        q_scr[2 * h] = q[:, :PEER_HALF]
        q_scr[2 * h + 1] = q[:, PEER_HALF:]

    key_tag = lax.broadcasted_iota(jnp.int32, (PEER_KEYS, ROUTE_SUB), 0).astype(jnp.float32)
    sub8 = lax.broadcasted_iota(jnp.int32, (8, ROUTE_SUB), 0).astype(jnp.float32)
    sub16 = lax.broadcasted_iota(jnp.int32, (PEER_TOPK, ROUTE_SUB), 0).astype(jnp.float32)

    pos = [sub16] + [sub8 + float(a * PEER_TOPK) for a in range(1, 8)] + [(sub8 + 8.0) * PEER_TOPK]
    pos = jnp.concatenate(pos, axis=0)

    def candidates(g):
        v1, i1, v2, i2 = v_scr[2 * g], i_scr[2 * g], v_scr[2 * g + 1], i_scr[2 * g + 1]
        cand = [v1[0:1] + v2]
        exp_id = [i1[0:1] * PEER_KEYS + i2]
        for a in range(1, 8):
            n_b = PEER_TOPK // (a + 1)
            c = v1[a:a + 1] + v2[0:8]
            cand.append(c if n_b >= 8 else jnp.where(sub8 < n_b, c, _NEG))
            exp_id.append(i1[a:a + 1] * PEER_KEYS + i2[0:8])
        cand.append(v1[8:16] + v2[0:1])
        exp_id.append(i1[8:16] * PEER_KEYS + i2[0:1])
        return jnp.concatenate(cand, axis=0), jnp.concatenate(exp_id, axis=0)

    def head_pair(jh, carry):
        j = jh // (PEER_HEADS // 2)
        h0 = 2 * (jh % (PEER_HEADS // 2))
        tok = pl.ds(pl.multiple_of(j * ROUTE_SUB, ROUTE_SUB), ROUTE_SUB)
        for g in range(2):
            s_t = []
            for p in range(2):
                hp = 2 * (h0 + g) + p
                q_hi, q_lo = _split_bf16(q_scr[hp, tok, :])
                s_t.append(_dot3(key_hi_ref[hp], key_lo_ref[hp], q_hi, q_lo, _NT))
            _extract_top(s_t, key_tag, PEER_TOPK,
                         [(v_scr.at[2 * g + p], i_scr.at[2 * g + p]) for p in range(2)])
        cands, exp_ids = zip(*(candidates(g) for g in range(2)))
        _extract_top(cands, pos, PEER_TOPK, [(top_scr.at[g], e_scr.at[g]) for g in range(2)], payloads=exp_ids)
        for g in range(2):
            top = top_scr[g]
            ex = jnp.exp(top - top[0:1])
            rows = pl.ds(pl.multiple_of((h0 + g) * PEER_TOPK, PEER_TOPK), PEER_TOPK)
            gate_t[rows, :] = ex / jnp.sum(ex, axis=0, keepdims=True)
            idx_t[rows, :] = e_scr[g] * float(ROW_SUBLANES)

        @pl.when(h0 == PEER_HEADS - 2)
        def _():
            idx_ref[tok, :] = idx_t[...].T.astype(jnp.int32)
            gate_ref[tok, :] = gate_t[...].T

        return carry

    lax.fori_loop(0, (rb // ROUTE_SUB) * (PEER_HEADS // 2), head_pair, 0)


def _peer_route(h, wq_hi, wq_lo, key_hi, key_lo):
    n_tok = h.shape[0]
    rb = ROUTE_BLOCK
    full = lambda a: pl.BlockSpec(a.shape, lambda i: (0,) * a.ndim)
    t16 = pltpu.VMEM((2, PEER_TOPK, ROUTE_SUB), jnp.float32)
    return pl.pallas_call(
        _peer_route_kernel,
        grid=(n_tok // rb,),
        in_specs=[pl.BlockSpec((rb, D_MODEL), lambda i: (i, 0)), full(wq_hi), full(wq_lo), full(key_hi), full(key_lo)],
        out_specs=[pl.BlockSpec((rb, PEER_HK), lambda i: (i, 0))] * 2,
        out_shape=[jax.ShapeDtypeStruct((n_tok, PEER_HK), jnp.int32),
                   jax.ShapeDtypeStruct((n_tok, PEER_HK), jnp.float32)],
        scratch_shapes=[pltpu.VMEM((2 * PEER_HEADS, rb, PEER_HALF), jnp.float32),
                        pltpu.VMEM((4, PEER_TOPK, ROUTE_SUB), jnp.float32),
                        pltpu.VMEM((4, PEER_TOPK, ROUTE_SUB), jnp.float32),
                        t16, t16,
                        pltpu.VMEM((PEER_HK, ROUTE_SUB), jnp.float32),
                        pltpu.VMEM((PEER_HK, ROUTE_SUB), jnp.float32)],
        compiler_params=pltpu.CompilerParams(dimension_semantics=("arbitrary",),
                                             vmem_limit_bytes=PEER_VMEM_LIMIT),
        name="peer_route",
    )(h, wq_hi, wq_lo, key_hi, key_lo)


def _peer_experts(h, idx, gates, tbl_u, tbl_v):
    n_tok = h.shape[0]
    n_exp = tbl_u.shape[0]
    tb = PEER_TOKEN_BLOCK
    grid = (n_tok // tb,)
    smem_spec = pl.BlockSpec((tb, PEER_HK), lambda i: (i, 0), memory_space=pltpu.SMEM)
    row_spec = pl.BlockSpec((tb, 2, ROW_SUBLANES, LANES), lambda i: (i, 0, 0, 0))
    params = pltpu.CompilerParams(dimension_semantics=("arbitrary",), vmem_limit_bytes=PEER_VMEM_LIMIT)
    z = pl.pallas_call(
        _peer_z_kernel,
        grid=grid,
        in_specs=[smem_spec, row_spec, _table_spec(n_exp)],
        out_specs=pl.BlockSpec((tb, PEER_HK), lambda i: (i, 0)),
        out_shape=jax.ShapeDtypeStruct((n_tok, PEER_HK), jnp.float32),
        scratch_shapes=[pltpu.VMEM((ROW_SUBLANES * _P_STRIDE, LANES), jnp.float32)] * 2,
        compiler_params=params,
        name="peer_z",
    )(idx, h.reshape(n_tok, 2, ROW_SUBLANES, LANES), tbl_u)
    vec_spec = pl.BlockSpec((tb, PEER_HK), lambda i: (i, 0))
    out = pl.pallas_call(
        _peer_out_kernel,
        grid=grid,
        in_specs=[smem_spec, vec_spec, vec_spec, _table_spec(n_exp)],
        out_specs=row_spec,
        out_shape=jax.ShapeDtypeStruct((n_tok, 2, ROW_SUBLANES, LANES), jnp.float32),
        scratch_shapes=[pltpu.VMEM((tb, PEER_HK), jnp.float32)] + [pltpu.VMEM((PEER_HK, LANES), jnp.float32)] * 2,
        compiler_params=params,
        name="peer_out",
    )(idx, z, gates, tbl_v)
    return out.reshape(n_tok, D_MODEL)


def _peer(h, w_query, sub_keys, expert_u, expert_v):
    wq_hi, wq_lo = _split_bf16(w_query)
    key_hi, key_lo = _split_bf16(sub_keys.reshape(2 * PEER_HEADS, PEER_KEYS, PEER_HALF))
    idx, gates = _peer_route(h, wq_hi, wq_lo, key_hi, key_lo)
    return _peer_experts(h, idx, gates, _pack_table(expert_u), _pack_table(expert_v))


def _rms_norm(x, g, eps=RMS_EPS):
    y = x * lax.rsqrt(jnp.mean(x * x, axis=-1, keepdims=True) + eps)
    return y * g


def _layer_norm(x, g, b, eps=LN_EPS):
    xc = x - jnp.mean(x, axis=-1, keepdims=True)
    y = xc * lax.rsqrt(jnp.mean(xc * xc, axis=-1, keepdims=True) + eps)
    return y * g + b


def _modulate(h, shift, scale):
    return h * (1.0 + scale) + shift


def _depthwise_conv(x, w):
    k = w.shape[0]
    return lax.conv_general_dilated(
        x, w[:, None, :], window_strides=(1,), padding=[(k // 2, k // 2)],
        dimension_numbers=('NWC', 'WIO', 'NWC'), feature_group_count=x.shape[-1])


def _short_conv_mixer(h, w_in, conv_w, w_out):
    b_gate, c_gate, u = jnp.split(h @ w_in, 3, axis=-1)
    return (b_gate * _depthwise_conv(c_gate * u, conv_w)) @ w_out


def _conformer_conv(h, w_pw1, b_pw1, dw_w, dw_b, ln_g, ln_b, w_pw2, b_pw2):
    a, g = jnp.split(h @ w_pw1 + b_pw1, 2, axis=-1)
    u = a * jax.nn.sigmoid(g)
    u = _depthwise_conv(u, dw_w) + dw_b
    u = jax.nn.silu(_layer_norm(u, ln_g, ln_b))
    return u @ w_pw2 + b_pw2


def _axial_rope_tables(n_tokens):
    rows = n_tokens // GRID_W
    row = jnp.repeat(jnp.arange(rows, dtype=jnp.float32), GRID_W)
    col = jnp.tile(jnp.arange(GRID_W, dtype=jnp.float32), rows)
    inv_freq = ROPE_BASE ** (-jnp.arange(ROPE_FREQS, dtype=jnp.float32) / ROPE_FREQS)
    ang = jnp.stack([row[:, None] * inv_freq, col[:, None] * inv_freq], axis=1)
    return jnp.cos(ang), jnp.sin(ang)


def _apply_axial_rope(t, cos, sin):
    tr = t.reshape(*t.shape[:-1], 2, 2, ROPE_FREQS)
    a, b = tr[..., 0, :], tr[..., 1, :]
    cs, sn = cos[None, :, None, None], sin[None, :, None, None]
    out = jnp.stack([a * cs - b * sn, a * sn + b * cs], axis=-2)
    return out.reshape(t.shape)


def _diff_softmax_attend(q1, q2, k1, k2, v, lam):
    scale = DA_HEAD_DIM ** -0.5
    s1 = jnp.einsum('bhqd,bhkd->bhqk', q1, k1, preferred_element_type=jnp.float32) * scale
    s2 = jnp.einsum('bhqd,bhkd->bhqk', q2, k2, preferred_element_type=jnp.float32) * scale
    p = jax.nn.softmax(s1, axis=-1) - lam * jax.nn.softmax(s2, axis=-1)
    return jnp.einsum('bhqk,bhkd->bhqd', p, v)


def _diff_attention(a_lat, a_ctx, w_qkv, q_norm_g, k_norm_g, lam_q1, lam_k1, lam_q2, lam_k2,
                    subln_g, w_o, lambda_init, with_ctx_out):
    bsz, n_lat, d = a_lat.shape
    lam = (jnp.exp(jnp.sum(lam_q1 * lam_k1)) - jnp.exp(jnp.sum(lam_q2 * lam_k2)) + lambda_init)

    def project(h):
        b, n, _ = h.shape
        q, k, v = jnp.split(h @ w_qkv, 3, axis=-1)
        q = _rms_norm(q.reshape(b, n, DA_HEADS, 2, DA_HEAD_DIM), q_norm_g)
        k = _rms_norm(k.reshape(b, n, DA_HEADS, 2, DA_HEAD_DIM), k_norm_g)
        return q, k, v.reshape(b, n, DA_HEADS, DA_V_DIM).transpose(0, 2, 1, 3)

    def heads(t, comp):
        return t[:, :, :, comp].transpose(0, 2, 1, 3)

    def finish(o):
        o = _rms_norm(o, subln_g) * (1.0 - lambda_init)
        return o.reshape(o.shape[0], o.shape[1], d) @ w_o

    q_l, k_l, v_l = project(a_lat)
    cos, sin = _axial_rope_tables(n_lat)
    q_l = _apply_axial_rope(q_l, cos, sin)
    k_l = _apply_axial_rope(k_l, cos, sin)
    q_c, k_c, v_c = project(a_ctx)
    k1 = jnp.concatenate([heads(k_l, 0), heads(k_c, 0)], axis=2)
    k2 = jnp.concatenate([heads(k_l, 1), heads(k_c, 1)], axis=2)
    v = jnp.concatenate([v_l, v_c], axis=2)
    n_blk = n_lat // Q_BLOCK

    def blocks(t):
        return t.reshape(bsz, DA_HEADS, n_blk, Q_BLOCK, DA_HEAD_DIM).transpose(2, 0, 1, 3, 4)

    o = lax.map(lambda qb: _diff_softmax_attend(qb[0], qb[1], k1, k2, v, lam),
                (blocks(heads(q_l, 0)), blocks(heads(q_l, 1))))
    o_lat = o.transpose(1, 0, 3, 2, 4).reshape(bsz, n_lat, DA_HEADS, DA_V_DIM)
    y_lat = finish(o_lat)
    if not with_ctx_out:
        return y_lat, None
    o_ctx = _diff_softmax_attend(heads(q_c, 0), heads(q_c, 1), heads(k_c, 0), heads(k_c, 1), v_c, lam)
    return y_lat, finish(o_ctx.transpose(0, 2, 1, 3))


def kernel(x, c, ctx, c_ctx, w_mod, b_mod, norm1_g, norm2_g, sc_w_in, sc_conv_w, sc_w_out, da_w_qkv, da_q_norm_g, da_k_norm_g, da_lam_q1, da_lam_k1, da_lam_q2, da_lam_k2, da_subln_g, da_w_o, cf_w_pw1, cf_b_pw1, cf_dw_w, cf_dw_b, cf_ln_g, cf_ln_b, cf_w_pw2, cf_b_pw2, peer_w_query, peer_sub_keys, peer_u, peer_v):
    bsz, n_lat, d = x.shape
    n_ctx = ctx.shape[1]
    ctx_read = [i for i in range(DEPTH) if i % N_MIXERS == 1]
    last_ctx_read = ctx_read[-1] if ctx_read else -1
    s_lat = jax.nn.silu(c)
    s_ctx = jax.nn.silu(c_ctx)[None]
    h_lat, h_ctx = x, ctx
    for i in range(DEPTH):
        kind, j = i % N_MIXERS, i // N_MIXERS
        use_ctx = i <= last_ctx_read
        adv_ctx = i < last_ctx_read
        m_lat = jnp.split((s_lat @ w_mod[i] + b_mod[i])[:, None, :], 6, axis=-1)
        a_lat = _modulate(_rms_norm(h_lat, norm1_g[i]), m_lat[0], m_lat[1])
        if use_ctx:
            m_ctx = jnp.split((s_ctx @ w_mod[i] + b_mod[i])[:, None, :], 6, axis=-1)
            a_ctx = _modulate(_rms_norm(h_ctx, norm1_g[i]), m_ctx[0], m_ctx[1])
        if kind == 0:
            y_lat = _short_conv_mixer(a_lat, sc_w_in[j], sc_conv_w[j], sc_w_out[j])
            if adv_ctx:
                y_ctx = _short_conv_mixer(a_ctx, sc_w_in[j], sc_conv_w[j], sc_w_out[j])
        elif kind == 1:
            y_lat, y_ctx = _diff_attention(a_lat, a_ctx, da_w_qkv[j], da_q_norm_g[j], da_k_norm_g[j],
                                           da_lam_q1[j], da_lam_k1[j], da_lam_q2[j], da_lam_k2[j],
                                           da_subln_g[j], da_w_o[j], 0.8 - 0.6 * math.exp(-0.3 * i), adv_ctx)
        else:
            y_lat = _conformer_conv(a_lat, cf_w_pw1[j], cf_b_pw1[j], cf_dw_w[j], cf_dw_b[j],
                                    cf_ln_g[j], cf_ln_b[j], cf_w_pw2[j], cf_b_pw2[j])
            if adv_ctx:
                y_ctx = _conformer_conv(a_ctx, cf_w_pw1[j], cf_b_pw1[j], cf_dw_w[j], cf_dw_b[j],
                                        cf_ln_g[j], cf_ln_b[j], cf_w_pw2[j], cf_b_pw2[j])
        h_lat = h_lat + m_lat[2] * y_lat
        f_lat = _modulate(_rms_norm(h_lat, norm2_g[i]), m_lat[3], m_lat[4]).reshape(-1, d)
        if adv_ctx:
            h_ctx = h_ctx + m_ctx[2] * y_ctx
            f_ctx = _modulate(_rms_norm(h_ctx, norm2_g[i]), m_ctx[3], m_ctx[4]).reshape(-1, d)
            f_all = jnp.concatenate([f_lat, f_ctx], axis=0)
        else:
            f_all = f_lat
        y_all = _peer(f_all, peer_w_query[i], peer_sub_keys[i], peer_u[i], peer_v[i])
        h_lat = h_lat + m_lat[5] * y_all[:bsz * n_lat].reshape(bsz, n_lat, d)
        if adv_ctx:
            h_ctx = h_ctx + m_ctx[5] * y_all[bsz * n_lat:].reshape(bsz, n_ctx, d)
    return h_lat
```

```python
import functools
import math

import jax
import jax.numpy as jnp
from jax import lax
from jax.experimental import pallas as pl
from jax.experimental.pallas import tpu as pltpu

D_MODEL = 1024
DEPTH = 4
GRID_W = 64
N_MIXERS = 3
DA_HEADS = 8
DA_HEAD_DIM = D_MODEL // (2 * DA_HEADS)
DA_V_DIM = 2 * DA_HEAD_DIM
ROPE_FREQS = DA_HEAD_DIM // 4
ROPE_BASE = 10000.0
Q_BLOCK = 128
PEER_HEADS = 8
PEER_KEYS = 128
PEER_EXPERTS = PEER_KEYS * PEER_KEYS
PEER_QDIM = 256
PEER_HALF = PEER_QDIM // 2
PEER_TOPK = 16
PEER_HK = PEER_HEADS * PEER_TOPK
RMS_EPS = 1e-6
LN_EPS = 1e-5

LANES = 128
HALF_D = D_MODEL // 2
ROW_SUBLANES = HALF_D // LANES
PEER_TOKEN_BLOCK = 128
PEER_VMEM_LIMIT = 50 * 1024 * 1024
ROW_BLOCK = 256
DENSE_VMEM_LIMIT = 48 * 1024 * 1024


def _group_of(i, block_rows, geo):
    bsz, n_lat, _ = geo
    return jnp.minimum((i * block_rows) // n_lat, bsz)


def _seq_edges(i, block_rows, geo):
    bsz, n_lat, n_ctx = geo
    row0 = i * block_rows
    in_lat = row0 < bsz * n_lat
    pos = jnp.where(in_lat, row0 % n_lat, (row0 - bsz * n_lat) % n_ctx)
    length = jnp.where(in_lat, n_lat, n_ctx)
    return pos == 0, pos + block_rows == length


def _pack_table(tbl):
    bits = lax.bitcast_convert_type(tbl.astype(jnp.bfloat16), jnp.uint16).astype(jnp.uint32)
    words = bits[:, :HALF_D] | (bits[:, HALF_D:] << 16)
    return lax.bitcast_convert_type(words, jnp.int32).reshape(tbl.shape[0] * ROW_SUBLANES, LANES)


def _table_row(tbl_ref, first_sublane):
    return tbl_ref[pl.ds(pl.multiple_of(first_sublane, ROW_SUBLANES), ROW_SUBLANES), :]


def _unpack(w):
    lo = lax.bitcast_convert_type(w << 16, jnp.float32)
    hi = lax.bitcast_convert_type(w & jnp.int32(-65536), jnp.float32)
    return lo, hi


_P_STRIDE = PEER_HK + 1


def _peer_z_kernel(idx_ref, h_ref, tbl_ref, z_ref, p_a, p_b, p_c, p_d):
    tb = z_ref.shape[0]
    ones = jnp.ones((8, 2 * LANES), jnp.bfloat16)

    def gather(t, p_scr):
        hl = h_ref[t, 0]
        hh = h_ref[t, 1]
        rows = idx_ref.at[t]
        for k in range(PEER_HK):
            lo, hi = _unpack(_table_row(tbl_ref, rows[k]))
            p_scr[pl.ds(k, ROW_SUBLANES, stride=_P_STRIDE), :] = lo * hl + hi * hh

    def reduce(t, p_scr):
        ps = p_scr[pl.ds(0, PEER_HK), :]
        for s in range(1, ROW_SUBLANES):
            ps = ps + p_scr[pl.ds(s * _P_STRIDE, PEER_HK), :]
        ps_hi = ps.astype(jnp.bfloat16)
        ps_lo = (ps - ps_hi.astype(jnp.float32)).astype(jnp.bfloat16)
        z = lax.dot_general(ones, jnp.concatenate([ps_hi, ps_lo], axis=1), (((1,), (1,)), ((), ())),
                            preferred_element_type=jnp.float32)
        z_ref[pl.ds(t, 1), :] = z[0:1]

    gather(0, p_a)
    gather(1, p_b)

    def token_quad(i, carry):
        t = 4 * i
        reduce(t, p_a)
        reduce(t + 1, p_b)
        gather(t + 2, p_c)
        gather(t + 3, p_d)
        reduce(t + 2, p_c)
        reduce(t + 3, p_d)
        gather(jnp.minimum(t + 4, tb - 1), p_a)
        gather(jnp.minimum(t + 5, tb - 1), p_b)
        return carry

    lax.fori_loop(0, tb // 4, token_quad, 0)


def _peer_out_kernel(idx_ref, z_ref, gate_ref, tbl_ref, h_ref, m_ref, o_ref, act_ref, b_a, b_b, b_c, b_d):
    tb = o_ref.shape[0]
    n_acc = 2
    z = z_ref[...]
    act_ref[...] = 0.5 * z * (1.0 + lax.erf(z * (2.0 ** -0.5))) * gate_ref[...]

    def spread(t, b_scr):
        b_scr[...] = jnp.broadcast_to(act_ref[pl.ds(t, 1), :], (PEER_HK, LANES)).T

    def accumulate(t, b_scr):
        acc_lo = [jnp.zeros((ROW_SUBLANES, LANES), jnp.float32) for _ in range(n_acc)]
        acc_hi = [jnp.zeros((ROW_SUBLANES, LANES), jnp.float32) for _ in range(n_acc)]
        rows = idx_ref.at[t]
        for k in range(PEER_HK):
            lo, hi = _unpack(_table_row(tbl_ref, rows[k]))
            a = jnp.broadcast_to(b_scr[pl.ds(k, 1), :], (ROW_SUBLANES, LANES))
            acc_lo[k % n_acc] = acc_lo[k % n_acc] + a * lo
            acc_hi[k % n_acc] = acc_hi[k % n_acc] + a * hi
        o_ref[t, 0] = h_ref[t, 0] + m_ref[0, 0, 0] * (acc_lo[0] + acc_lo[1])
        o_ref[t, 1] = h_ref[t, 1] + m_ref[0, 0, 1] * (acc_hi[0] + acc_hi[1])

    spread(0, b_a)
    spread(1, b_b)

    def token_quad(i, carry):
        t = 4 * i
        spread(t + 2, b_c)
        spread(t + 3, b_d)
        accumulate(t, b_a)
        accumulate(t + 1, b_b)
        spread(jnp.minimum(t + 4, tb - 1), b_a)
        spread(jnp.minimum(t + 5, tb - 1), b_b)
        accumulate(t + 2, b_c)
        accumulate(t + 3, b_d)
        return carry

    lax.fori_loop(0, tb // 4, token_quad, 0)


def _table_spec(n_rows):
    return pl.BlockSpec((n_rows, LANES), lambda i: (0, 0), pipeline_mode=pl.Buffered(1))


def _split_bf16(x):
    hi = x.astype(jnp.bfloat16)
    return hi, (x - hi.astype(jnp.float32)).astype(jnp.bfloat16)


def _dot3(a_hi, a_lo, b_hi, b_lo, dims):
    dot = functools.partial(lax.dot_general, dimension_numbers=dims, preferred_element_type=jnp.float32)
    return dot(a_hi, b_hi) + (dot(a_lo, b_hi) + dot(a_hi, b_lo))


_NN = (((1,), (0,)), ((), ()))
_NT = (((1,), (1,)), ((), ()))
ROUTE_BLOCK = 512
ROUTE_SUB = LANES
_NEG = -jnp.inf


def _extract_top(scores, tag, n_top, out_refs, payloads=None):
    scores = list(scores)
    for r in range(n_top):
        for i, score in enumerate(scores):
            m = jnp.max(score, axis=0, keepdims=True)
            at = jnp.min(jnp.where(score == m, tag, 1e9), axis=0, keepdims=True)
            sel = tag == at
            out_refs[i][0][r:r + 1, :] = m
            if payloads is None:
                out_refs[i][1][r:r + 1, :] = at
            else:
                out_refs[i][1][r:r + 1, :] = jnp.max(jnp.where(sel, payloads[i], -1.0), axis=0, keepdims=True)
            scores[i] = jnp.where(sel, _NEG, score)


def _peer_route_kernel(f_ref, wq_hi_ref, wq_lo_ref, key_hi_ref, key_lo_ref, idx_ref, gate_ref,
                       q_scr, v_scr, i_scr, top_scr, e_scr, idx_t, gate_t):
    rb = f_ref.shape[0]
    f_hi, f_lo = _split_bf16(f_ref[...])
    for h in range(PEER_HEADS):
        cols = slice(h * PEER_QDIM, (h + 1) * PEER_QDIM)
        q = _dot3(f_hi, f_lo, wq_hi_ref[:, cols], wq_lo_ref[:, cols], _NN)
        q_scr[2 * h] = q[:, :PEER_HALF]
        q_scr[2 * h + 1] = q[:, PEER_HALF:]

    key_tag = lax.broadcasted_iota(jnp.int32, (PEER_KEYS, ROUTE_SUB), 0).astype(jnp.float32)
    sub8 = lax.broadcasted_iota(jnp.int32, (8, ROUTE_SUB), 0).astype(jnp.float32)
    sub16 = lax.broadcasted_iota(jnp.int32, (PEER_TOPK, ROUTE_SUB), 0).astype(jnp.float32)

    pos = [sub16] + [sub8 + float(a * PEER_TOPK) for a in range(1, 8)] + [(sub8 + 8.0) * PEER_TOPK]
    pos = jnp.concatenate(pos, axis=0)

    def candidates(g):
        v1, i1, v2, i2 = v_scr[2 * g], i_scr[2 * g], v_scr[2 * g + 1], i_scr[2 * g + 1]
        cand = [v1[0:1] + v2]
        exp_id = [i1[0:1] * PEER_KEYS + i2]
        for a in range(1, 8):
            n_b = PEER_TOPK // (a + 1)
            c = v1[a:a + 1] + v2[0:8]
            cand.append(c if n_b >= 8 else jnp.where(sub8 < n_b, c, _NEG))
            exp_id.append(i1[a:a + 1] * PEER_KEYS + i2[0:8])
        cand.append(v1[8:16] + v2[0:1])
        exp_id.append(i1[8:16] * PEER_KEYS + i2[0:1])
        return jnp.concatenate(cand, axis=0), jnp.concatenate(exp_id, axis=0)

    def head_pair(jh, carry):
        j = jh // (PEER_HEADS // 2)
        h0 = 2 * (jh % (PEER_HEADS // 2))
        tok = pl.ds(pl.multiple_of(j * ROUTE_SUB, ROUTE_SUB), ROUTE_SUB)
        for g in range(2):
            s_t = []
            for p in range(2):
                hp = 2 * (h0 + g) + p
                q_hi, q_lo = _split_bf16(q_scr[hp, tok, :])
                s_t.append(_dot3(key_hi_ref[hp], key_lo_ref[hp], q_hi, q_lo, _NT))
            _extract_top(s_t, key_tag, PEER_TOPK,
                         [(v_scr.at[2 * g + p], i_scr.at[2 * g + p]) for p in range(2)])
        cands, exp_ids = zip(*(candidates(g) for g in range(2)))
        _extract_top(cands, pos, PEER_TOPK, [(top_scr.at[g], e_scr.at[g]) for g in range(2)], payloads=exp_ids)
        for g in range(2):
            top = top_scr[g]
            ex = jnp.exp(top - top[0:1])
            rows = pl.ds(pl.multiple_of((h0 + g) * PEER_TOPK, PEER_TOPK), PEER_TOPK)
            gate_t[rows, :] = ex / jnp.sum(ex, axis=0, keepdims=True)
            idx_t[rows, :] = e_scr[g] * float(ROW_SUBLANES)

        @pl.when(h0 == PEER_HEADS - 2)
        def _():
            idx_ref[tok, :] = idx_t[...].T.astype(jnp.int32)
            gate_ref[tok, :] = gate_t[...].T

        return carry

    lax.fori_loop(0, (rb // ROUTE_SUB) * (PEER_HEADS // 2), head_pair, 0)


def _peer_route(h, wq_hi, wq_lo, key_hi, key_lo):
    n_tok = h.shape[0]
    rb = ROUTE_BLOCK
    full = lambda a: pl.BlockSpec(a.shape, lambda i: (0,) * a.ndim)
    t16 = pltpu.VMEM((2, PEER_TOPK, ROUTE_SUB), jnp.float32)
    return pl.pallas_call(
        _peer_route_kernel,
        grid=(n_tok // rb,),
        in_specs=[pl.BlockSpec((rb, D_MODEL), lambda i: (i, 0)), full(wq_hi), full(wq_lo), full(key_hi), full(key_lo)],
        out_specs=[pl.BlockSpec((rb, PEER_HK), lambda i: (i, 0))] * 2,
        out_shape=[jax.ShapeDtypeStruct((n_tok, PEER_HK), jnp.int32),
                   jax.ShapeDtypeStruct((n_tok, PEER_HK), jnp.float32)],
        scratch_shapes=[pltpu.VMEM((2 * PEER_HEADS, rb, PEER_HALF), jnp.float32),
                        pltpu.VMEM((4, PEER_TOPK, ROUTE_SUB), jnp.float32),
                        pltpu.VMEM((4, PEER_TOPK, ROUTE_SUB), jnp.float32),
                        t16, t16,
                        pltpu.VMEM((PEER_HK, ROUTE_SUB), jnp.float32),
                        pltpu.VMEM((PEER_HK, ROUTE_SUB), jnp.float32)],
        compiler_params=pltpu.CompilerParams(dimension_semantics=("arbitrary",),
                                             vmem_limit_bytes=PEER_VMEM_LIMIT),
        name="peer_route",
    )(h, wq_hi, wq_lo, key_hi, key_lo)


def _peer_experts(f, idx, gates, tbl_u, tbl_v, h, m, geo):
    n_tok = f.shape[0]
    n_exp = tbl_u.shape[0]
    tb = PEER_TOKEN_BLOCK
    grid = (n_tok // tb,)
    smem_spec = pl.BlockSpec((tb, PEER_HK), lambda i: (i, 0), memory_space=pltpu.SMEM)
    row_spec = pl.BlockSpec((tb, 2, ROW_SUBLANES, LANES), lambda i: (i, 0, 0, 0))
    gate_spec = pl.BlockSpec((1, 1, 2, ROW_SUBLANES, LANES), lambda i: (_group_of(i, tb, geo), 5, 0, 0, 0))
    params = pltpu.CompilerParams(dimension_semantics=("arbitrary",), vmem_limit_bytes=PEER_VMEM_LIMIT)
    z = pl.pallas_call(
        _peer_z_kernel,
        grid=grid,
        in_specs=[smem_spec, row_spec, _table_spec(n_exp)],
        out_specs=pl.BlockSpec((tb, PEER_HK), lambda i: (i, 0)),
        out_shape=jax.ShapeDtypeStruct((n_tok, PEER_HK), jnp.float32),
        scratch_shapes=[pltpu.VMEM((ROW_SUBLANES * _P_STRIDE, LANES), jnp.float32)] * 4,
        compiler_params=params,
        name="peer_z",
    )(idx, f.reshape(n_tok, 2, ROW_SUBLANES, LANES), tbl_u)
    vec_spec = pl.BlockSpec((tb, PEER_HK), lambda i: (i, 0))
    out = pl.pallas_call(
        _peer_out_kernel,
        grid=grid,
        in_specs=[smem_spec, vec_spec, vec_spec, _table_spec(n_exp), row_spec, gate_spec],
        out_specs=row_spec,
        out_shape=jax.ShapeDtypeStruct((n_tok, 2, ROW_SUBLANES, LANES), jnp.float32),
        scratch_shapes=[pltpu.VMEM((tb, PEER_HK), jnp.float32)] + [pltpu.VMEM((PEER_HK, LANES), jnp.float32)] * 4,
        compiler_params=params,
        name="peer_out",
    )(idx, z, gates, tbl_v, h.reshape(h.shape[0], 2, ROW_SUBLANES, LANES),
      m.reshape(m.shape[0], 6, 2, ROW_SUBLANES, LANES))
    return out.reshape(n_tok, D_MODEL)


def _peer(f, h, m, geo, w_query, sub_keys, expert_u, expert_v):
    wq_hi, wq_lo = _split_bf16(w_query)
    key_hi, key_lo = _split_bf16(sub_keys.reshape(2 * PEER_HEADS, PEER_KEYS, PEER_HALF))
    idx, gates = _peer_route(f, wq_hi, wq_lo, key_hi, key_lo)
    return _peer_experts(f, idx, gates, _pack_table(expert_u), _pack_table(expert_v), h, m, geo)


def _dense_params():
    return pltpu.CompilerParams(dimension_semantics=("arbitrary",), vmem_limit_bytes=DENSE_VMEM_LIMIT)


def _full_spec(a):
    return pl.BlockSpec(a.shape, lambda *_: (0,) * a.ndim)


def _mod_spec(block_rows, geo):
    return pl.BlockSpec((1, 6, D_MODEL), lambda i: (_group_of(i, block_rows, geo), 0, 0))


def _norm_mod(x, g, shift, scale):
    y = x * lax.rsqrt(jnp.mean(x * x, axis=-1, keepdims=True) + RMS_EPS)
    return (y * g) * (1.0 + scale) + shift


def _adaln_kernel(c_ref, w_ref, b_ref, o_ref):
    c = c_ref[...]
    s_hi, s_lo = _split_bf16(c * jax.nn.sigmoid(c))
    w_hi, w_lo = _split_bf16(w_ref[0])
    o_ref[0] = _dot3(s_hi, s_lo, w_hi, w_lo, _NN) + b_ref[0]


def _adaln(c, c_ctx, w_mod, b_mod):
    depth, d, n_out = w_mod.shape
    n_groups = c.shape[0] + 1
    pad = -n_groups % 8
    c_all = jnp.concatenate([c, c_ctx[None], jnp.zeros((pad, d), c.dtype)], axis=0)
    tn = D_MODEL
    out = pl.pallas_call(
        _adaln_kernel,
        grid=(depth, n_out // tn),
        in_specs=[pl.BlockSpec(c_all.shape, lambda l, n: (0, 0)),
                  pl.BlockSpec((1, d, tn), lambda l, n: (l, 0, n)),
                  pl.BlockSpec((1, 1, tn), lambda l, n: (l, 0, n))],
        out_specs=pl.BlockSpec((1, c_all.shape[0], tn), lambda l, n: (l, 0, n)),
        out_shape=jax.ShapeDtypeStruct((depth, c_all.shape[0], n_out), jnp.float32),
        compiler_params=pltpu.CompilerParams(dimension_semantics=("arbitrary", "arbitrary"),
                                             vmem_limit_bytes=DENSE_VMEM_LIMIT),
        name="adaln",
    )(c_all, w_mod, b_mod.reshape(depth, 1, n_out))
    return out[:, :n_groups].reshape(depth, n_groups, 6, D_MODEL)


def _pre_kernel(h_ref, m_ref, g_ref, w_ref, b_ref, o_ref, *, glu):
    a = _norm_mod(h_ref[...], g_ref[...], m_ref[0, 0:1, :], m_ref[0, 1:2, :])
    y = jnp.dot(a.astype(jnp.bfloat16), w_ref[...], preferred_element_type=jnp.float32) + b_ref[...]
    if glu:
        half = y.shape[1] // 2
        y = y[:, :half] * jax.nn.sigmoid(y[:, half:])
    o_ref[...] = y


def _pre(h, m, g, w, b, rows, geo, glu=False):
    tm = ROW_BLOCK
    n_out = w.shape[1]
    b = jnp.zeros((n_out,), jnp.float32) if b is None else b
    width = n_out // 2 if glu else n_out
    g2, b2, w16 = g.reshape(1, -1), b.reshape(1, -1), w.astype(jnp.bfloat16)
    return pl.pallas_call(
        functools.partial(_pre_kernel, glu=glu),
        grid=(rows // tm,),
        in_specs=[pl.BlockSpec((tm, D_MODEL), lambda i: (i, 0)), _mod_spec(tm, geo),
                  _full_spec(g2), _full_spec(w16), _full_spec(b2)],
        out_specs=pl.BlockSpec((tm, width), lambda i: (i, 0)),
        out_shape=jax.ShapeDtypeStruct((rows, width), jnp.float32),
        compiler_params=_dense_params(),
        name="mixer_in",
    )(h, m, g2, w16, b2)


def _post_kernel(x_ref, w_ref, b_ref, h_ref, m_ref, g_ref, hn_ref, f_ref):
    y = jnp.dot(x_ref[...].astype(jnp.bfloat16), w_ref[...], preferred_element_type=jnp.float32) + b_ref[...]
    hn = h_ref[...] + m_ref[0, 2:3, :] * y
    hn_ref[...] = hn
    f_ref[...] = _norm_mod(hn, g_ref[...], m_ref[0, 3:4, :], m_ref[0, 4:5, :])


def _post(x, w, b, h, m, g, rows, geo):
    tm = ROW_BLOCK
    b = jnp.zeros((D_MODEL,), jnp.float32) if b is None else b
    g2, b2, w16 = g.reshape(1, -1), b.reshape(1, -1), w.astype(jnp.bfloat16)
    row_spec = pl.BlockSpec((tm, D_MODEL), lambda i: (i, 0))
    return pl.pallas_call(
        _post_kernel,
        grid=(rows // tm,),
        in_specs=[row_spec, _full_spec(w16), _full_spec(b2), row_spec, _mod_spec(tm, geo), _full_spec(g2)],
        out_specs=[row_spec, row_spec],
        out_shape=[jax.ShapeDtypeStruct((rows, D_MODEL), jnp.float32)] * 2,
        compiler_params=_dense_params(),
        name="mixer_out",
    )(x, w16, b2, h, m, g2)


def _halo_specs(tm, halo, width, n_rows):
    per = tm // halo
    last = n_rows // halo - 1
    prev_spec = pl.BlockSpec((halo, width), lambda i: (jnp.maximum(i * per - 1, 0), 0))
    next_spec = pl.BlockSpec((halo, width), lambda i: (jnp.minimum((i + 1) * per, last), 0))
    return prev_spec, next_spec


SC_HALO = 8
CF_HALO = 16
CF_WIDTH = 31


def _short_conv_kernel(y_ref, prev_ref, next_ref, w_ref, o_ref, *, geo):
    tm = y_ref.shape[0]
    d = D_MODEL
    first, last = _seq_edges(pl.program_id(0), tm, geo)
    v = y_ref[:, d:2 * d] * y_ref[:, 2 * d:]
    before = prev_ref[SC_HALO - 1:SC_HALO, d:2 * d] * prev_ref[SC_HALO - 1:SC_HALO, 2 * d:]
    after = next_ref[0:1, d:2 * d] * next_ref[0:1, 2 * d:]
    before = jnp.where(first, 0.0, before)
    after = jnp.where(last, 0.0, after)
    row = lax.broadcasted_iota(jnp.int32, (tm, 1), 0)
    up = jnp.where(row == 0, before, pltpu.roll(v, 1, axis=0))
    down = jnp.where(row == tm - 1, after, pltpu.roll(v, tm - 1, axis=0))
    conv = w_ref[0:1, :] * up + w_ref[1:2, :] * v + w_ref[2:3, :] * down
    o_ref[...] = y_ref[:, :d] * conv


def _short_conv(y3, conv_w, geo):
    rows = y3.shape[0]
    tm = ROW_BLOCK
    prev_spec, next_spec = _halo_specs(tm, SC_HALO, 3 * D_MODEL, rows)
    return pl.pallas_call(
        functools.partial(_short_conv_kernel, geo=geo),
        grid=(rows // tm,),
        in_specs=[pl.BlockSpec((tm, 3 * D_MODEL), lambda i: (i, 0)), prev_spec, next_spec, _full_spec(conv_w)],
        out_specs=pl.BlockSpec((tm, D_MODEL), lambda i: (i, 0)),
        out_shape=jax.ShapeDtypeStruct((rows, D_MODEL), jnp.float32),
        compiler_params=_dense_params(),
        name="short_conv",
    )(y3, y3, y3, conv_w)


def _conf_conv_kernel(u_ref, prev_ref, next_ref, w_ref, b_ref, g_ref, beta_ref, o_ref, pad_scr, *, geo):
    tm = u_ref.shape[0]
    first, last = _seq_edges(pl.program_id(0), tm, geo)
    pad_scr[0:CF_HALO, :] = jnp.where(first, 0.0, prev_ref[...])
    pad_scr[CF_HALO:CF_HALO + tm, :] = u_ref[...]
    pad_scr[CF_HALO + tm:, :] = jnp.where(last, 0.0, next_ref[...])
    shift = CF_HALO - CF_WIDTH // 2
    acc = w_ref[0:1, :] * pad_scr[shift:shift + tm, :]
    for j in range(1, CF_WIDTH):
        acc = acc + w_ref[j:j + 1, :] * pad_scr[shift + j:shift + j + tm, :]
    y = acc + b_ref[...]
    yc = y - jnp.mean(y, axis=-1, keepdims=True)
    ln = yc * lax.rsqrt(jnp.mean(yc * yc, axis=-1, keepdims=True) + LN_EPS) * g_ref[...] + beta_ref[...]
    o_ref[...] = ln * jax.nn.sigmoid(ln)


def _conf_conv(u, dw_w, dw_b, ln_g, ln_b, geo):
    rows = u.shape[0]
    tm = ROW_BLOCK
    prev_spec, next_spec = _halo_specs(tm, CF_HALO, D_MODEL, rows)
    vecs = [v.reshape(1, -1) for v in (dw_b, ln_g, ln_b)]
    return pl.pallas_call(
        functools.partial(_conf_conv_kernel, geo=geo),
        grid=(rows // tm,),
        in_specs=[pl.BlockSpec((tm, D_MODEL), lambda i: (i, 0)), prev_spec, next_spec, _full_spec(dw_w)]
        + [_full_spec(v) for v in vecs],
        out_specs=pl.BlockSpec((tm, D_MODEL), lambda i: (i, 0)),
        out_shape=jax.ShapeDtypeStruct((rows, D_MODEL), jnp.float32),
        scratch_shapes=[pltpu.VMEM((tm + 2 * CF_HALO, D_MODEL), jnp.float32)],
        compiler_params=_dense_params(),
        name="conformer_conv",
    )(u, u, u, dw_w, *vecs)


HEAD_W = 2 * DA_HEAD_DIM
ROPE_PAIR = ROPE_FREQS


def _rope_tables(n_lat):
    pos = jnp.arange(n_lat, dtype=jnp.float32)
    lane = jnp.arange(HEAD_W)
    within = lane % DA_HEAD_DIM
    coord = jnp.where((within // (2 * ROPE_FREQS) == 0)[None, :], (pos // GRID_W)[:, None], (pos % GRID_W)[:, None])
    inv_freq = ROPE_BASE ** (-(within % ROPE_FREQS).astype(jnp.float32) / ROPE_FREQS)
    ang = coord * inv_freq[None, :]
    first_of_pair = ((within // ROPE_FREQS) % 2 == 0)[None, :]
    sin = jnp.sin(ang)
    return jnp.cos(ang), jnp.where(first_of_pair, -sin, 0.0), jnp.where(first_of_pair, 0.0, sin)


def _qk_prep_kernel(qkv_ref, gq_ref, gk_ref, ones_ref, cos_ref, sa_ref, sb_ref, q_ref, k_ref, v_ref, *, geo):
    bsz, n_lat, _ = geo
    d = D_MODEL
    tm = qkv_ref.shape[0]
    is_lat = pl.program_id(0) * tm < bsz * n_lat
    cos = jnp.where(is_lat, cos_ref[...], 1.0)
    sin_a = jnp.where(is_lat, sa_ref[...], 0.0)
    sin_b = jnp.where(is_lat, sb_ref[...], 0.0)

    def prep(x, g, out_scale):
        sq_hi, sq_lo = _split_bf16(x * x)
        ssq = (jnp.dot(sq_hi, ones_ref[...], preferred_element_type=jnp.float32)
               + jnp.dot(sq_lo, ones_ref[...], preferred_element_type=jnp.float32))
        xn = x * lax.rsqrt(ssq * (1.0 / DA_HEAD_DIM) + RMS_EPS) * g
        heads = []
        for h in range(DA_HEADS):
            t = xn[:, h * HEAD_W:(h + 1) * HEAD_W]
            heads.append(t * cos + pltpu.roll(t, HEAD_W - ROPE_PAIR, axis=1) * sin_a
                         + pltpu.roll(t, ROPE_PAIR, axis=1) * sin_b)
        return (jnp.concatenate(heads, axis=1) * out_scale).astype(jnp.bfloat16)

    q_ref[...] = prep(qkv_ref[:, :d], gq_ref[...], DA_HEAD_DIM ** -0.5)
    k_ref[...] = prep(qkv_ref[:, d:2 * d], gk_ref[...], 1.0)
    v_ref[...] = qkv_ref[:, 2 * d:].astype(jnp.bfloat16)


def _qk_prep(qkv, q_norm_g, k_norm_g, geo):
    bsz, n_lat, _ = geo
    rows = qkv.shape[0]
    tm = ROW_BLOCK
    gq = jnp.tile(q_norm_g, D_MODEL // DA_HEAD_DIM).reshape(1, -1)
    gk = jnp.tile(k_norm_g, D_MODEL // DA_HEAD_DIM).reshape(1, -1)
    group = jnp.arange(D_MODEL) // DA_HEAD_DIM
    ones = (group[:, None] == group[None, :]).astype(jnp.bfloat16)
    tables = _rope_tables(n_lat)
    per_seq = n_lat // tm
    table_spec = pl.BlockSpec((tm, HEAD_W), lambda i: (i % per_seq, 0))
    row_spec = pl.BlockSpec((tm, D_MODEL), lambda i: (i, 0))
    return pl.pallas_call(
        functools.partial(_qk_prep_kernel, geo=geo),
        grid=(rows // tm,),
        in_specs=[pl.BlockSpec((tm, 3 * D_MODEL), lambda i: (i, 0)), _full_spec(gq), _full_spec(gk), _full_spec(ones)]
        + [table_spec] * 3,
        out_specs=[row_spec] * 3,
        out_shape=[jax.ShapeDtypeStruct((rows, D_MODEL), jnp.bfloat16)] * 3,
        compiler_params=_dense_params(),
        name="qk_prep",
    )(qkv, gq, gk, ones, *tables)


ATTN_Q_BLOCK = 512


def _diff_attn_kernel(lam_ref, q_ref, kl_ref, kc_ref, vl_ref, vc_ref, g_ref, o_ref, *, out_scale):
    q = q_ref[...]
    lane = lax.broadcasted_iota(jnp.int32, (1, HEAD_W), 1)
    zero = jnp.zeros((), q.dtype)

    def attend(qm):
        s_l = lax.dot_general(qm, kl_ref[...], _NT, preferred_element_type=jnp.float32)
        s_c = lax.dot_general(qm, kc_ref[...], _NT, preferred_element_type=jnp.float32)
        m = jnp.maximum(jnp.max(s_l, axis=-1, keepdims=True), jnp.max(s_c, axis=-1, keepdims=True))
        e_l = jnp.exp(s_l - m)
        e_c = jnp.exp(s_c - m)
        denom = jnp.sum(e_l, axis=-1, keepdims=True) + jnp.sum(e_c, axis=-1, keepdims=True)
        o = (jnp.dot(e_l.astype(jnp.bfloat16), vl_ref[...], preferred_element_type=jnp.float32)
             + jnp.dot(e_c.astype(jnp.bfloat16), vc_ref[...], preferred_element_type=jnp.float32))
        return o / denom

    o = attend(jnp.where(lane < DA_HEAD_DIM, q, zero)) - lam_ref[0] * attend(jnp.where(lane >= DA_HEAD_DIM, q, zero))
    o = o * lax.rsqrt(jnp.mean(o * o, axis=-1, keepdims=True) + RMS_EPS)
    o_ref[...] = o * g_ref[...] * out_scale


def _diff_attn(q, k, v, lam, subln_g, out_scale, geo):
    bsz, n_lat, n_ctx = geo
    tq = ATTN_Q_BLOCK
    nq = n_lat // tq
    ctx0 = bsz * n_lat // n_ctx
    q_spec = pl.BlockSpec((tq, HEAD_W), lambda b, h, i: (b * nq + i, h))
    lat_spec = pl.BlockSpec((n_lat, HEAD_W), lambda b, h, i: (b, h))
    ctx_spec = pl.BlockSpec((n_ctx, HEAD_W), lambda b, h, i: (ctx0 + b, h))
    g = subln_g.reshape(1, -1)
    return pl.pallas_call(
        functools.partial(_diff_attn_kernel, out_scale=out_scale),
        grid=(bsz, DA_HEADS, nq),
        in_specs=[pl.BlockSpec(memory_space=pltpu.SMEM), q_spec, lat_spec, ctx_spec, lat_spec, ctx_spec,
                  pl.BlockSpec(g.shape, lambda b, h, i: (0, 0))],
        out_specs=q_spec,
        out_shape=jax.ShapeDtypeStruct((bsz * n_lat, D_MODEL), jnp.float32),
        compiler_params=pltpu.CompilerParams(dimension_semantics=("arbitrary",) * 3,
                                             vmem_limit_bytes=DENSE_VMEM_LIMIT),
        name="diff_attn",
    )(lam.reshape(1), q, k, k, v, v, g)


def kernel(x, c, ctx, c_ctx, w_mod, b_mod, norm1_g, norm2_g, sc_w_in, sc_conv_w, sc_w_out, da_w_qkv, da_q_norm_g, da_k_norm_g, da_lam_q1, da_lam_k1, da_lam_q2, da_lam_k2, da_subln_g, da_w_o, cf_w_pw1, cf_b_pw1, cf_dw_w, cf_dw_b, cf_ln_g, cf_ln_b, cf_w_pw2, cf_b_pw2, peer_w_query, peer_sub_keys, peer_u, peer_v):
    bsz, n_lat, d = x.shape
    n_ctx = ctx.shape[1]
    geo = (bsz, n_lat, n_ctx)
    assert d == D_MODEL and n_lat % ATTN_Q_BLOCK == 0 and n_lat % ROW_BLOCK == 0 and n_ctx % ROW_BLOCK == 0
    assert (bsz * n_lat) % ROUTE_BLOCK == 0 and (bsz * n_ctx) % ROUTE_BLOCK == 0 and n_lat % GRID_W == 0
    lat_rows = bsz * n_lat
    all_rows = lat_rows + bsz * n_ctx
    ctx_read = [i for i in range(DEPTH) if i % N_MIXERS == 1]
    last_ctx_read = ctx_read[-1] if ctx_read else -1
    m_all = _adaln(c, c_ctx, w_mod, b_mod)
    h = jnp.concatenate([x.reshape(lat_rows, d), ctx.reshape(all_rows - lat_rows, d)], axis=0)
    for i in range(DEPTH):
        kind, j = i % N_MIXERS, i // N_MIXERS
        m = m_all[i]
        rows = all_rows if i < last_ctx_read else lat_rows
        if kind == 0:
            y3 = _pre(h, m, norm1_g[i], sc_w_in[j], None, rows, geo)
            mixed = _short_conv(y3, sc_conv_w[j], geo)
            w_out, b_out = sc_w_out[j], None
        elif kind == 1:
            assert rows == lat_rows
            lambda_init = 0.8 - 0.6 * math.exp(-0.3 * i)
            lam = (jnp.exp(jnp.sum(da_lam_q1[j] * da_lam_k1[j])) - jnp.exp(jnp.sum(da_lam_q2[j] * da_lam_k2[j]))
                   + lambda_init)
            qkv = _pre(h, m, norm1_g[i], da_w_qkv[j], None, all_rows, geo)
            q, k, v = _qk_prep(qkv, da_q_norm_g[j], da_k_norm_g[j], geo)
            mixed = _diff_attn(q, k, v, lam, da_subln_g[j], 1.0 - lambda_init, geo)
            w_out, b_out = da_w_o[j], None
        else:
            u = _pre(h, m, norm1_g[i], cf_w_pw1[j], cf_b_pw1[j], rows, geo, glu=True)
            mixed = _conf_conv(u, cf_dw_w[j], cf_dw_b[j], cf_ln_g[j], cf_ln_b[j], geo)
            w_out, b_out = cf_w_pw2[j], cf_b_pw2[j]
        h_mid, f = _post(mixed, w_out, b_out, h, m, norm2_g[i], rows, geo)
        h = _peer(f, h_mid, m, geo, peer_w_query[i], peer_sub_keys[i], peer_u[i], peer_v[i])
    return h[:lat_rows].reshape(bsz, n_lat, d)
```

```python
import functools
import math

import jax
import jax.numpy as jnp
from jax import lax
from jax.experimental import pallas as pl
from jax.experimental.pallas import tpu as pltpu

D_MODEL = 1024
DEPTH = 4
GRID_W = 64
N_MIXERS = 3
DA_HEADS = 8
DA_HEAD_DIM = D_MODEL // (2 * DA_HEADS)
DA_V_DIM = 2 * DA_HEAD_DIM
ROPE_FREQS = DA_HEAD_DIM // 4
ROPE_BASE = 10000.0
Q_BLOCK = 128
PEER_HEADS = 8
PEER_KEYS = 128
PEER_EXPERTS = PEER_KEYS * PEER_KEYS
PEER_QDIM = 256
PEER_HALF = PEER_QDIM // 2
PEER_TOPK = 16
PEER_HK = PEER_HEADS * PEER_TOPK
RMS_EPS = 1e-6
LN_EPS = 1e-5

LANES = 128
HALF_D = D_MODEL // 2
ROW_SUBLANES = HALF_D // LANES
PEER_TOKEN_BLOCK = 128
PEER_VMEM_LIMIT = 50 * 1024 * 1024
ROW_BLOCK = 256
DENSE_VMEM_LIMIT = 48 * 1024 * 1024


def _group_of(i, block_rows, geo):
    bsz, n_lat, _ = geo
    return jnp.minimum((i * block_rows) // n_lat, bsz)


def _seq_edges(i, block_rows, geo):
    bsz, n_lat, n_ctx = geo
    row0 = i * block_rows
    in_lat = row0 < bsz * n_lat
    pos = jnp.where(in_lat, row0 % n_lat, (row0 - bsz * n_lat) % n_ctx)
    length = jnp.where(in_lat, n_lat, n_ctx)
    return pos == 0, pos + block_rows == length


def _pack_table(tbl):
    bits = lax.bitcast_convert_type(tbl.astype(jnp.bfloat16), jnp.uint16).astype(jnp.uint32)
    words = bits[:, :HALF_D] | (bits[:, HALF_D:] << 16)
    return lax.bitcast_convert_type(words, jnp.int32).reshape(tbl.shape[0] * ROW_SUBLANES, LANES)


def _table_row(tbl_ref, first_sublane):
    return tbl_ref[pl.ds(pl.multiple_of(first_sublane, ROW_SUBLANES), ROW_SUBLANES), :]


def _unpack(w):
    lo = lax.bitcast_convert_type(w << 16, jnp.float32)
    hi = lax.bitcast_convert_type(w & jnp.int32(-65536), jnp.float32)
    return lo, hi


_G_STRIDE = PEER_HK + 1
IDX_OFFSETS = 8


def _token_pipeline(n_tokens, fill, drain, bufs, fill_leads):
    b_a, b_b, b_c, b_d = bufs
    fill(0, b_a)
    fill(1, b_b)

    def half(t, cur, nxt, last_half):
        ahead = [jnp.minimum(t + 2 + n, n_tokens - 1) if last_half else t + 2 + n for n in range(2)]
        fills = lambda: [fill(ahead[n], nxt[n]) for n in range(2)]
        drains = lambda: [drain(t + n, cur[n]) for n in range(2)]
        for emit in ((fills, drains) if fill_leads else (drains, fills)):
            emit()

    def token_quad(i, carry):
        half(4 * i, (b_a, b_b), (b_c, b_d), False)
        half(4 * i + 2, (b_c, b_d), (b_a, b_b), True)
        return carry

    lax.fori_loop(0, n_tokens // 4, token_quad, 0)


def _expert_rows(offs, idx_ref, tbl_ref, t):
    for c in range(PEER_HK // IDX_OFFSETS):
        rows = idx_ref.at[t, pl.ds(c * IDX_OFFSETS, IDX_OFFSETS)]
        for kk in range(IDX_OFFSETS):
            yield (c * IDX_OFFSETS + kk,) + _unpack(_table_row(tbl_ref, rows[offs[kk]]))


def _token_half(t, half):
    return pl.ds(pl.multiple_of(t * (2 * ROW_SUBLANES) + half * ROW_SUBLANES, ROW_SUBLANES), ROW_SUBLANES)


def _peer_z_kernel(offs_ref, idx_ref, f_ref, tbl_ref, z_ref, *bufs):
    offs = [offs_ref[i] for i in range(IDX_OFFSETS)]
    ones = jnp.ones((8, 2 * LANES), jnp.bfloat16)

    def products(t, p_scr):
        f_lo = f_ref[_token_half(t, 0), :]
        f_hi = f_ref[_token_half(t, 1), :]
        for k, lo, hi in _expert_rows(offs, idx_ref, tbl_ref, t):
            p_scr[pl.ds(k, ROW_SUBLANES, stride=_G_STRIDE), :] = lo * f_lo + hi * f_hi

    def reduce(t, p_scr):
        ps = p_scr[pl.ds(0, PEER_HK), :]
        for s in range(1, ROW_SUBLANES):
            ps = ps + p_scr[pl.ds(s * _G_STRIDE, PEER_HK), :]
        ps_hi, ps_lo = _split_bf16(ps)
        z = lax.dot_general(ones, jnp.concatenate([ps_hi, ps_lo], axis=1), _NT, preferred_element_type=jnp.float32)
        z_ref[pl.ds(t, 1), :] = z[0:1]

    _token_pipeline(z_ref.shape[0], products, reduce, bufs, fill_leads=False)


def _peer_out_kernel(idx_ref, z_ref, gate_ref, tbl_ref, h_ref, m_ref, o_ref, act_ref, *bufs):
    n_acc = 2
    z = z_ref[...]
    act_ref[...] = 0.5 * z * (1.0 + lax.erf(z * (2.0 ** -0.5))) * gate_ref[...]
    m_tile = jnp.concatenate([m_ref[0, 0, 0], m_ref[0, 0, 1]], axis=0)

    def spread(t, b_scr):
        b_scr[...] = jnp.broadcast_to(act_ref[pl.ds(t, 1), :], (PEER_HK, LANES)).T

    def accumulate(t, b_scr):
        acc_lo = [jnp.zeros((ROW_SUBLANES, LANES), jnp.float32) for _ in range(n_acc)]
        acc_hi = [jnp.zeros((ROW_SUBLANES, LANES), jnp.float32) for _ in range(n_acc)]
        rows = idx_ref.at[t]
        for k in range(PEER_HK):
            lo, hi = _unpack(_table_row(tbl_ref, rows[k]))
            a = jnp.broadcast_to(b_scr[pl.ds(k, 1), :], (ROW_SUBLANES, LANES))
            acc_lo[k % n_acc] = acc_lo[k % n_acc] + a * lo
            acc_hi[k % n_acc] = acc_hi[k % n_acc] + a * hi
        total = jnp.concatenate([acc_lo[0] + acc_lo[1], acc_hi[0] + acc_hi[1]], axis=0)
        rows = pl.ds(pl.multiple_of(t * LANE_TILES, LANE_TILES), LANE_TILES)
        o_ref[rows, :] = h_ref[rows, :] + m_tile * total

    _token_pipeline(z_ref.shape[0], spread, accumulate, bufs, fill_leads=True)


def _table_spec(n_rows):
    return pl.BlockSpec((n_rows, LANES), lambda i: (0, 0), pipeline_mode=pl.Buffered(1))


def _split_bf16(x):
    hi = x.astype(jnp.bfloat16)
    return hi, (x - hi.astype(jnp.float32)).astype(jnp.bfloat16)


def _dot3(a_hi, a_lo, b_hi, b_lo, dims):
    dot = functools.partial(lax.dot_general, dimension_numbers=dims, preferred_element_type=jnp.float32)
    return dot(a_hi, b_hi) + (dot(a_lo, b_hi) + dot(a_hi, b_lo))


_NN = (((1,), (0,)), ((), ()))
_NT = (((1,), (1,)), ((), ()))
ROUTE_BLOCK = 512
ROUTE_SUB = LANES
ROUTE_LOCKSTEP = 4
_NEG = -jnp.inf


def _extract_top(scores, tag, n_top, out_refs, payloads=None):
    scores = list(scores)
    for r in range(n_top):
        for i, score in enumerate(scores):
            m = jnp.max(score, axis=0, keepdims=True)
            at = jnp.min(jnp.where(score == m, tag, 1e9), axis=0, keepdims=True)
            sel = tag == at
            out_refs[i][0][r:r + 1, :] = m
            if payloads is None:
                out_refs[i][1][r:r + 1, :] = at
            else:
                out_refs[i][1][r:r + 1, :] = jnp.max(jnp.where(sel, payloads[i], -1.0), axis=0, keepdims=True)
            scores[i] = jnp.where(sel, _NEG, score)


def _peer_route_kernel(f_ref, wq_hi_ref, wq_lo_ref, key_hi_ref, key_lo_ref, idx_ref, gate_ref,
                       q_scr, v_scr, i_scr, top_scr, e_scr, idx_t, gate_t):
    rb = f_ref.shape[0]
    f_hi, f_lo = _split_bf16(f_ref[...])
    for h in range(PEER_HEADS):
        cols = slice(h * PEER_QDIM, (h + 1) * PEER_QDIM)
        q = _dot3(f_hi, f_lo, wq_hi_ref[:, cols], wq_lo_ref[:, cols], _NN)
        q_scr[2 * h] = q[:, :PEER_HALF]
        q_scr[2 * h + 1] = q[:, PEER_HALF:]

    key_tag = lax.broadcasted_iota(jnp.int32, (PEER_KEYS, ROUTE_SUB), 0).astype(jnp.float32)
    sub8 = lax.broadcasted_iota(jnp.int32, (8, ROUTE_SUB), 0).astype(jnp.float32)
    sub16 = lax.broadcasted_iota(jnp.int32, (PEER_TOPK, ROUTE_SUB), 0).astype(jnp.float32)

    pos = [sub16] + [sub8 + float(a * PEER_TOPK) for a in range(1, 8)] + [(sub8 + 8.0) * PEER_TOPK]
    pos = jnp.concatenate(pos, axis=0)

    def candidates(h):
        v1, i1, v2, i2 = v_scr[2 * h], i_scr[2 * h], v_scr[2 * h + 1], i_scr[2 * h + 1]
        cand = [v1[0:1] + v2]
        exp_id = [i1[0:1] * PEER_KEYS + i2]
        for a in range(1, 8):
            n_b = PEER_TOPK // (a + 1)
            c = v1[a:a + 1] + v2[0:8]
            cand.append(c if n_b >= 8 else jnp.where(sub8 < n_b, c, _NEG))
            exp_id.append(i1[a:a + 1] * PEER_KEYS + i2[0:8])
        cand.append(v1[8:16] + v2[0:1])
        exp_id.append(i1[8:16] * PEER_KEYS + i2[0:1])
        return jnp.concatenate(cand, axis=0), jnp.concatenate(exp_id, axis=0)

    def token_tile(j, carry):
        tok = pl.ds(pl.multiple_of(j * ROUTE_SUB, ROUTE_SUB), ROUTE_SUB)

        def half_keys(grp, c):
            hps = [grp * ROUTE_LOCKSTEP + n for n in range(ROUTE_LOCKSTEP)]
            s_t = []
            for hp in hps:
                q_hi, q_lo = _split_bf16(q_scr[hp, tok, :])
                s_t.append(_dot3(key_hi_ref[hp], key_lo_ref[hp], q_hi, q_lo, _NT))
            _extract_top(s_t, key_tag, PEER_TOPK, [(v_scr.at[hp], i_scr.at[hp]) for hp in hps])
            return c

        lax.fori_loop(0, 2 * PEER_HEADS // ROUTE_LOCKSTEP, half_keys, 0)

        def products(grp, c):
            heads = [grp * ROUTE_LOCKSTEP + n for n in range(ROUTE_LOCKSTEP)]
            cands, exp_ids = zip(*(candidates(h) for h in heads))
            _extract_top(cands, pos, PEER_TOPK, [(top_scr.at[n], e_scr.at[n]) for n in range(ROUTE_LOCKSTEP)],
                         payloads=exp_ids)
            for n, h in enumerate(heads):
                top = top_scr[n]
                ex = jnp.exp(top - top[0:1])
                rows = pl.ds(pl.multiple_of(h * PEER_TOPK, PEER_TOPK), PEER_TOPK)
                gate_t[rows, :] = ex / jnp.sum(ex, axis=0, keepdims=True)
                idx_t[rows, :] = e_scr[n] * float(ROW_SUBLANES)
            return c

        lax.fori_loop(0, PEER_HEADS // ROUTE_LOCKSTEP, products, 0)
        idx_ref[tok, :] = idx_t[...].T.astype(jnp.int32)
        gate_ref[tok, :] = gate_t[...].T
        return carry

    lax.fori_loop(0, rb // ROUTE_SUB, token_tile, 0)


def _peer_route(h, wq_hi, wq_lo, key_hi, key_lo):
    n_tok = h.shape[0]
    rb = ROUTE_BLOCK
    full = lambda a: pl.BlockSpec(a.shape, lambda i: (0,) * a.ndim)
    t16 = pltpu.VMEM((ROUTE_LOCKSTEP, PEER_TOPK, ROUTE_SUB), jnp.float32)
    return pl.pallas_call(
        _peer_route_kernel,
        grid=(n_tok // rb,),
        in_specs=[pl.BlockSpec((rb, D_MODEL), lambda i: (i, 0)), full(wq_hi), full(wq_lo), full(key_hi), full(key_lo)],
        out_specs=[pl.BlockSpec((rb, PEER_HK), lambda i: (i, 0))] * 2,
        out_shape=[jax.ShapeDtypeStruct((n_tok, PEER_HK), jnp.int32),
                   jax.ShapeDtypeStruct((n_tok, PEER_HK), jnp.float32)],
        scratch_shapes=[pltpu.VMEM((2 * PEER_HEADS, rb, PEER_HALF), jnp.float32),
                        pltpu.VMEM((2 * PEER_HEADS, PEER_TOPK, ROUTE_SUB), jnp.float32),
                        pltpu.VMEM((2 * PEER_HEADS, PEER_TOPK, ROUTE_SUB), jnp.float32),
                        t16, t16,
                        pltpu.VMEM((PEER_HK, ROUTE_SUB), jnp.float32),
                        pltpu.VMEM((PEER_HK, ROUTE_SUB), jnp.float32)],
        compiler_params=pltpu.CompilerParams(dimension_semantics=("arbitrary",),
                                             vmem_limit_bytes=PEER_VMEM_LIMIT),
        name="peer_route",
    )(h, wq_hi, wq_lo, key_hi, key_lo)


def _peer_experts(f_tiled, idx, gates, tbl_u, tbl_v, h_tiled, m, geo):
    n_tok = idx.shape[0]
    n_exp = tbl_u.shape[0]
    tb = PEER_TOKEN_BLOCK
    grid = (n_tok // tb,)
    offs = jnp.arange(IDX_OFFSETS, dtype=jnp.int32)
    offs_spec = pl.BlockSpec(memory_space=pltpu.SMEM)
    smem_spec = pl.BlockSpec((tb, PEER_HK), lambda i: (i, 0), memory_space=pltpu.SMEM)
    gate_spec = pl.BlockSpec((1, 1, 2, ROW_SUBLANES, LANES), lambda i: (_group_of(i, tb, geo), 5, 0, 0, 0))
    vec_spec = pl.BlockSpec((tb, PEER_HK), lambda i: (i, 0))
    products = [pltpu.VMEM((ROW_SUBLANES * _G_STRIDE, LANES), jnp.float32)] * 4
    spreads = [pltpu.VMEM((PEER_HK, LANES), jnp.float32)] * 4
    params = pltpu.CompilerParams(dimension_semantics=("arbitrary",), vmem_limit_bytes=PEER_VMEM_LIMIT)
    z = pl.pallas_call(
        _peer_z_kernel,
        grid=grid,
        in_specs=[offs_spec, smem_spec, _tiled_spec(tb), _table_spec(n_exp)],
        out_specs=vec_spec,
        out_shape=jax.ShapeDtypeStruct((n_tok, PEER_HK), jnp.float32),
        scratch_shapes=products,
        compiler_params=params,
        name="peer_z",
    )(offs, idx, f_tiled, tbl_u)
    return pl.pallas_call(
        _peer_out_kernel,
        grid=grid,
        in_specs=[smem_spec, vec_spec, vec_spec, _table_spec(n_exp), _tiled_spec(tb), gate_spec],
        out_specs=_tiled_spec(tb),
        out_shape=jax.ShapeDtypeStruct((n_tok * LANE_TILES, LANES), jnp.float32),
        scratch_shapes=[pltpu.VMEM((tb, PEER_HK), jnp.float32)] + spreads,
        compiler_params=params,
        name="peer_out",
    )(idx, z, gates, tbl_v, h_tiled, m.reshape(m.shape[0], 6, 2, ROW_SUBLANES, LANES))


def _peer(f, f_tiled, h_tiled, m, geo, w_query, sub_keys, expert_u, expert_v):
    wq_hi, wq_lo = _split_bf16(w_query)
    key_hi, key_lo = _split_bf16(sub_keys.reshape(2 * PEER_HEADS, PEER_KEYS, PEER_HALF))
    idx, gates = _peer_route(f, wq_hi, wq_lo, key_hi, key_lo)
    return _peer_experts(f_tiled, idx, gates, _pack_table(expert_u), _pack_table(expert_v), h_tiled, m, geo)


def _dense_params():
    return pltpu.CompilerParams(dimension_semantics=("arbitrary",), vmem_limit_bytes=DENSE_VMEM_LIMIT)


def _full_spec(a):
    return pl.BlockSpec(a.shape, lambda *_: (0,) * a.ndim)


def _mod_spec(block_rows, geo):
    return pl.BlockSpec((1, 6, D_MODEL), lambda i: (_group_of(i, block_rows, geo), 0, 0))


def _norm_mod(x, g, shift, scale):
    y = x * lax.rsqrt(jnp.mean(x * x, axis=-1, keepdims=True) + RMS_EPS)
    return (y * g) * (1.0 + scale) + shift


def _adaln_kernel(c_ref, w_ref, b_ref, o_ref):
    c = c_ref[...]
    s_hi, s_lo = _split_bf16(c * jax.nn.sigmoid(c))
    w_hi, w_lo = _split_bf16(w_ref[0])
    o_ref[0] = _dot3(s_hi, s_lo, w_hi, w_lo, _NN) + b_ref[0]


def _adaln(c, c_ctx, w_mod, b_mod):
    depth, d, n_out = w_mod.shape
    n_groups = c.shape[0] + 1
    pad = -n_groups % 8
    c_all = jnp.concatenate([c, c_ctx[None], jnp.zeros((pad, d), c.dtype)], axis=0)
    tn = D_MODEL
    out = pl.pallas_call(
        _adaln_kernel,
        grid=(depth, n_out // tn),
        in_specs=[pl.BlockSpec(c_all.shape, lambda l, n: (0, 0)),
                  pl.BlockSpec((1, d, tn), lambda l, n: (l, 0, n)),
                  pl.BlockSpec((1, 1, tn), lambda l, n: (l, 0, n))],
        out_specs=pl.BlockSpec((1, c_all.shape[0], tn), lambda l, n: (l, 0, n)),
        out_shape=jax.ShapeDtypeStruct((depth, c_all.shape[0], n_out), jnp.float32),
        compiler_params=pltpu.CompilerParams(dimension_semantics=("arbitrary", "arbitrary"),
                                             vmem_limit_bytes=DENSE_VMEM_LIMIT),
        name="adaln",
    )(c_all, w_mod, b_mod.reshape(depth, 1, n_out))
    return out[:, :n_groups].reshape(depth, n_groups, 6, D_MODEL)


LANE_TILES = D_MODEL // LANES


def _tiled_spec(block_tokens):
    return pl.BlockSpec((block_tokens * LANE_TILES, LANES), lambda i: (i, 0))


def _load_tokens(ref, tiled):
    if not tiled:
        return ref[...]
    n = ref.shape[0] // LANE_TILES
    return jnp.concatenate([ref[pl.ds(j, n, stride=LANE_TILES), :] for j in range(LANE_TILES)], axis=1)


def _store_tiled(ref, x):
    n = x.shape[0]
    for j in range(LANE_TILES):
        ref[pl.ds(j, n, stride=LANE_TILES), :] = x[:, j * LANES:(j + 1) * LANES]


def _pre_kernel(h_ref, m_ref, g_ref, w_ref, b_ref, o_ref, *, glu, tiled):
    a = _norm_mod(_load_tokens(h_ref, tiled), g_ref[...], m_ref[0, 0:1, :], m_ref[0, 1:2, :])
    y = jnp.dot(a.astype(jnp.bfloat16), w_ref[...], preferred_element_type=jnp.float32) + b_ref[...]
    if glu:
        half = y.shape[1] // 2
        y = y[:, :half] * jax.nn.sigmoid(y[:, half:])
    o_ref[...] = y


def _pre(h, m, g, w, b, rows, geo, tiled, glu=False):
    tm = ROW_BLOCK
    n_out = w.shape[1]
    b = jnp.zeros((n_out,), jnp.float32) if b is None else b
    width = n_out // 2 if glu else n_out
    g2, b2, w16 = g.reshape(1, -1), b.reshape(1, -1), w.astype(jnp.bfloat16)
    return pl.pallas_call(
        functools.partial(_pre_kernel, glu=glu, tiled=tiled),
        grid=(rows // tm,),
        in_specs=[_tiled_spec(tm) if tiled else pl.BlockSpec((tm, D_MODEL), lambda i: (i, 0)), _mod_spec(tm, geo),
                  _full_spec(g2), _full_spec(w16), _full_spec(b2)],
        out_specs=pl.BlockSpec((tm, width), lambda i: (i, 0)),
        out_shape=jax.ShapeDtypeStruct((rows, width), jnp.float32),
        compiler_params=_dense_params(),
        name="mixer_in",
    )(h, m, g2, w16, b2)


def _post_kernel(x_ref, w_ref, b_ref, h_ref, m_ref, g_ref, hn_ref, f_ref, ft_ref, *, tiled):
    y = jnp.dot(x_ref[...].astype(jnp.bfloat16), w_ref[...], preferred_element_type=jnp.float32) + b_ref[...]
    hn = _load_tokens(h_ref, tiled) + m_ref[0, 2:3, :] * y
    _store_tiled(hn_ref, hn)
    f = _norm_mod(hn, g_ref[...], m_ref[0, 3:4, :], m_ref[0, 4:5, :])
    f_ref[...] = f
    _store_tiled(ft_ref, f)


def _post(x, w, b, h, m, g, rows, geo, tiled):
    tm = ROW_BLOCK
    b = jnp.zeros((D_MODEL,), jnp.float32) if b is None else b
    g2, b2, w16 = g.reshape(1, -1), b.reshape(1, -1), w.astype(jnp.bfloat16)
    row_spec = pl.BlockSpec((tm, D_MODEL), lambda i: (i, 0))
    tiled_shape = jax.ShapeDtypeStruct((rows * LANE_TILES, LANES), jnp.float32)
    return pl.pallas_call(
        functools.partial(_post_kernel, tiled=tiled),
        grid=(rows // tm,),
        in_specs=[row_spec, _full_spec(w16), _full_spec(b2), _tiled_spec(tm) if tiled else row_spec,
                  _mod_spec(tm, geo), _full_spec(g2)],
        out_specs=[_tiled_spec(tm), row_spec, _tiled_spec(tm)],
        out_shape=[tiled_shape, jax.ShapeDtypeStruct((rows, D_MODEL), jnp.float32), tiled_shape],
        compiler_params=_dense_params(),
        name="mixer_out",
    )(x, w16, b2, h, m, g2)


def _halo_specs(tm, halo, width, n_rows):
    per = tm // halo
    last = n_rows // halo - 1
    prev_spec = pl.BlockSpec((halo, width), lambda i: (jnp.maximum(i * per - 1, 0), 0))
    next_spec = pl.BlockSpec((halo, width), lambda i: (jnp.minimum((i + 1) * per, last), 0))
    return prev_spec, next_spec


SC_HALO = 8
CF_HALO = 16
CF_WIDTH = 31


def _short_conv_kernel(y_ref, prev_ref, next_ref, w_ref, o_ref, *, geo):
    tm = y_ref.shape[0]
    d = D_MODEL
    first, last = _seq_edges(pl.program_id(0), tm, geo)
    v = y_ref[:, d:2 * d] * y_ref[:, 2 * d:]
    before = prev_ref[SC_HALO - 1:SC_HALO, d:2 * d] * prev_ref[SC_HALO - 1:SC_HALO, 2 * d:]
    after = next_ref[0:1, d:2 * d] * next_ref[0:1, 2 * d:]
    before = jnp.where(first, 0.0, before)
    after = jnp.where(last, 0.0, after)
    row = lax.broadcasted_iota(jnp.int32, (tm, 1), 0)
    up = jnp.where(row == 0, before, pltpu.roll(v, 1, axis=0))
    down = jnp.where(row == tm - 1, after, pltpu.roll(v, tm - 1, axis=0))
    conv = w_ref[0:1, :] * up + w_ref[1:2, :] * v + w_ref[2:3, :] * down
    o_ref[...] = y_ref[:, :d] * conv


def _short_conv(y3, conv_w, geo):
    rows = y3.shape[0]
    tm = ROW_BLOCK
    prev_spec, next_spec = _halo_specs(tm, SC_HALO, 3 * D_MODEL, rows)
    return pl.pallas_call(
        functools.partial(_short_conv_kernel, geo=geo),
        grid=(rows // tm,),
        in_specs=[pl.BlockSpec((tm, 3 * D_MODEL), lambda i: (i, 0)), prev_spec, next_spec, _full_spec(conv_w)],
        out_specs=pl.BlockSpec((tm, D_MODEL), lambda i: (i, 0)),
        out_shape=jax.ShapeDtypeStruct((rows, D_MODEL), jnp.float32),
        compiler_params=_dense_params(),
        name="short_conv",
    )(y3, y3, y3, conv_w)


def _conf_conv_kernel(u_ref, prev_ref, next_ref, w_ref, b_ref, g_ref, beta_ref, o_ref, pad_scr, *, geo):
    tm = u_ref.shape[0]
    first, last = _seq_edges(pl.program_id(0), tm, geo)
    pad_scr[0:CF_HALO, :] = jnp.where(first, 0.0, prev_ref[...])
    pad_scr[CF_HALO:CF_HALO + tm, :] = u_ref[...]
    pad_scr[CF_HALO + tm:, :] = jnp.where(last, 0.0, next_ref[...])
    shift = CF_HALO - CF_WIDTH // 2
    acc = w_ref[0:1, :] * pad_scr[shift:shift + tm, :]
    for j in range(1, CF_WIDTH):
        acc = acc + w_ref[j:j + 1, :] * pad_scr[shift + j:shift + j + tm, :]
    y = acc + b_ref[...]
    yc = y - jnp.mean(y, axis=-1, keepdims=True)
    ln = yc * lax.rsqrt(jnp.mean(yc * yc, axis=-1, keepdims=True) + LN_EPS) * g_ref[...] + beta_ref[...]
    o_ref[...] = ln * jax.nn.sigmoid(ln)


def _conf_conv(u, dw_w, dw_b, ln_g, ln_b, geo):
    rows = u.shape[0]
    tm = ROW_BLOCK
    prev_spec, next_spec = _halo_specs(tm, CF_HALO, D_MODEL, rows)
    vecs = [v.reshape(1, -1) for v in (dw_b, ln_g, ln_b)]
    return pl.pallas_call(
        functools.partial(_conf_conv_kernel, geo=geo),
        grid=(rows // tm,),
        in_specs=[pl.BlockSpec((tm, D_MODEL), lambda i: (i, 0)), prev_spec, next_spec, _full_spec(dw_w)]
        + [_full_spec(v) for v in vecs],
        out_specs=pl.BlockSpec((tm, D_MODEL), lambda i: (i, 0)),
        out_shape=jax.ShapeDtypeStruct((rows, D_MODEL), jnp.float32),
        scratch_shapes=[pltpu.VMEM((tm + 2 * CF_HALO, D_MODEL), jnp.float32)],
        compiler_params=_dense_params(),
        name="conformer_conv",
    )(u, u, u, dw_w, *vecs)


HEAD_W = 2 * DA_HEAD_DIM
ROPE_PAIR = ROPE_FREQS


def _rope_tables(n_lat):
    pos = jnp.arange(n_lat, dtype=jnp.float32)
    lane = jnp.arange(HEAD_W)
    within = lane % DA_HEAD_DIM
    coord = jnp.where((within // (2 * ROPE_FREQS) == 0)[None, :], (pos // GRID_W)[:, None], (pos % GRID_W)[:, None])
    inv_freq = ROPE_BASE ** (-(within % ROPE_FREQS).astype(jnp.float32) / ROPE_FREQS)
    ang = coord * inv_freq[None, :]
    first_of_pair = ((within // ROPE_FREQS) % 2 == 0)[None, :]
    sin = jnp.sin(ang)
    return jnp.cos(ang), jnp.where(first_of_pair, -sin, 0.0), jnp.where(first_of_pair, 0.0, sin)


def _qk_prep_kernel(qkv_ref, gq_ref, gk_ref, ones_ref, cos_ref, sa_ref, sb_ref, q_ref, k_ref, v_ref, *, geo):
    bsz, n_lat, _ = geo
    d = D_MODEL
    tm = qkv_ref.shape[0]
    is_lat = pl.program_id(0) * tm < bsz * n_lat
    cos = jnp.where(is_lat, cos_ref[...], 1.0)
    sin_a = jnp.where(is_lat, sa_ref[...], 0.0)
    sin_b = jnp.where(is_lat, sb_ref[...], 0.0)

    def prep(x, g, out_scale):
        sq_hi, sq_lo = _split_bf16(x * x)
        ssq = (jnp.dot(sq_hi, ones_ref[...], preferred_element_type=jnp.float32)
               + jnp.dot(sq_lo, ones_ref[...], preferred_element_type=jnp.float32))
        xn = x * lax.rsqrt(ssq * (1.0 / DA_HEAD_DIM) + RMS_EPS) * g
        heads = []
        for h in range(DA_HEADS):
            t = xn[:, h * HEAD_W:(h + 1) * HEAD_W]
            heads.append(t * cos + pltpu.roll(t, HEAD_W - ROPE_PAIR, axis=1) * sin_a
                         + pltpu.roll(t, ROPE_PAIR, axis=1) * sin_b)
        return (jnp.concatenate(heads, axis=1) * out_scale).astype(jnp.bfloat16)

    q_ref[...] = prep(qkv_ref[:, :d], gq_ref[...], DA_HEAD_DIM ** -0.5)
    k_ref[...] = prep(qkv_ref[:, d:2 * d], gk_ref[...], 1.0)
    v_ref[...] = qkv_ref[:, 2 * d:].astype(jnp.bfloat16)


def _qk_prep(qkv, q_norm_g, k_norm_g, geo):
    bsz, n_lat, _ = geo
    rows = qkv.shape[0]
    tm = ROW_BLOCK
    gq = jnp.tile(q_norm_g, D_MODEL // DA_HEAD_DIM).reshape(1, -1)
    gk = jnp.tile(k_norm_g, D_MODEL // DA_HEAD_DIM).reshape(1, -1)
    group = jnp.arange(D_MODEL) // DA_HEAD_DIM
    ones = (group[:, None] == group[None, :]).astype(jnp.bfloat16)
    tables = _rope_tables(n_lat)
    per_seq = n_lat // tm
    table_spec = pl.BlockSpec((tm, HEAD_W), lambda i: (i % per_seq, 0))
    row_spec = pl.BlockSpec((tm, D_MODEL), lambda i: (i, 0))
    return pl.pallas_call(
        functools.partial(_qk_prep_kernel, geo=geo),
        grid=(rows // tm,),
        in_specs=[pl.BlockSpec((tm, 3 * D_MODEL), lambda i: (i, 0)), _full_spec(gq), _full_spec(gk), _full_spec(ones)]
        + [table_spec] * 3,
        out_specs=[row_spec] * 3,
        out_shape=[jax.ShapeDtypeStruct((rows, D_MODEL), jnp.bfloat16)] * 3,
        compiler_params=_dense_params(),
        name="qk_prep",
    )(qkv, gq, gk, ones, *tables)


ATTN_Q_BLOCK = 512


def _diff_attn_kernel(lam_ref, q_ref, kl_ref, kc_ref, vl_ref, vc_ref, g_ref, o_ref, *, out_scale):
    q = q_ref[...]
    lane = lax.broadcasted_iota(jnp.int32, (1, HEAD_W), 1)
    zero = jnp.zeros((), q.dtype)

    def attend(qm):
        s_l = lax.dot_general(qm, kl_ref[...], _NT, preferred_element_type=jnp.float32)
        s_c = lax.dot_general(qm, kc_ref[...], _NT, preferred_element_type=jnp.float32)
        m = jnp.maximum(jnp.max(s_l, axis=-1, keepdims=True), jnp.max(s_c, axis=-1, keepdims=True))
        e_l = jnp.exp(s_l - m)
        e_c = jnp.exp(s_c - m)
        denom = jnp.sum(e_l, axis=-1, keepdims=True) + jnp.sum(e_c, axis=-1, keepdims=True)
        o = (jnp.dot(e_l.astype(jnp.bfloat16), vl_ref[...], preferred_element_type=jnp.float32)
             + jnp.dot(e_c.astype(jnp.bfloat16), vc_ref[...], preferred_element_type=jnp.float32))
        return o / denom

    o = attend(jnp.where(lane < DA_HEAD_DIM, q, zero)) - lam_ref[0] * attend(jnp.where(lane >= DA_HEAD_DIM, q, zero))
    o = o * lax.rsqrt(jnp.mean(o * o, axis=-1, keepdims=True) + RMS_EPS)
    o_ref[...] = o * g_ref[...] * out_scale


def _diff_attn(q, k, v, lam, subln_g, out_scale, geo):
    bsz, n_lat, n_ctx = geo
    tq = ATTN_Q_BLOCK
    nq = n_lat // tq
    ctx0 = bsz * n_lat // n_ctx
    q_spec = pl.BlockSpec((tq, HEAD_W), lambda b, h, i: (b * nq + i, h))
    lat_spec = pl.BlockSpec((n_lat, HEAD_W), lambda b, h, i: (b, h))
    ctx_spec = pl.BlockSpec((n_ctx, HEAD_W), lambda b, h, i: (ctx0 + b, h))
    g = subln_g.reshape(1, -1)
    return pl.pallas_call(
        functools.partial(_diff_attn_kernel, out_scale=out_scale),
        grid=(bsz, DA_HEADS, nq),
        in_specs=[pl.BlockSpec(memory_space=pltpu.SMEM), q_spec, lat_spec, ctx_spec, lat_spec, ctx_spec,
                  pl.BlockSpec(g.shape, lambda b, h, i: (0, 0))],
        out_specs=q_spec,
        out_shape=jax.ShapeDtypeStruct((bsz * n_lat, D_MODEL), jnp.float32),
        compiler_params=pltpu.CompilerParams(dimension_semantics=("arbitrary",) * 3,
                                             vmem_limit_bytes=DENSE_VMEM_LIMIT),
        name="diff_attn",
    )(lam.reshape(1), q, k, k, v, v, g)


def kernel(x, c, ctx, c_ctx, w_mod, b_mod, norm1_g, norm2_g, sc_w_in, sc_conv_w, sc_w_out, da_w_qkv, da_q_norm_g, da_k_norm_g, da_lam_q1, da_lam_k1, da_lam_q2, da_lam_k2, da_subln_g, da_w_o, cf_w_pw1, cf_b_pw1, cf_dw_w, cf_dw_b, cf_ln_g, cf_ln_b, cf_w_pw2, cf_b_pw2, peer_w_query, peer_sub_keys, peer_u, peer_v):
    bsz, n_lat, d = x.shape
    n_ctx = ctx.shape[1]
    geo = (bsz, n_lat, n_ctx)
    assert d == D_MODEL and n_lat % ATTN_Q_BLOCK == 0 and n_lat % ROW_BLOCK == 0 and n_ctx % ROW_BLOCK == 0
    assert (bsz * n_lat) % ROUTE_BLOCK == 0 and (bsz * n_ctx) % ROUTE_BLOCK == 0 and n_lat % GRID_W == 0
    lat_rows = bsz * n_lat
    all_rows = lat_rows + bsz * n_ctx
    ctx_read = [i for i in range(DEPTH) if i % N_MIXERS == 1]
    last_ctx_read = ctx_read[-1] if ctx_read else -1
    m_all = _adaln(c, c_ctx, w_mod, b_mod)
    h = jnp.concatenate([x.reshape(lat_rows, d), ctx.reshape(all_rows - lat_rows, d)], axis=0)
    for i in range(DEPTH):
        kind, j = i % N_MIXERS, i // N_MIXERS
        m = m_all[i]
        rows = all_rows if i < last_ctx_read else lat_rows
        tiled = i > 0
        if kind == 0:
            y3 = _pre(h, m, norm1_g[i], sc_w_in[j], None, rows, geo, tiled)
            mixed = _short_conv(y3, sc_conv_w[j], geo)
            w_out, b_out = sc_w_out[j], None
        elif kind == 1:
            assert rows == lat_rows
            lambda_init = 0.8 - 0.6 * math.exp(-0.3 * i)
            lam = (jnp.exp(jnp.sum(da_lam_q1[j] * da_lam_k1[j])) - jnp.exp(jnp.sum(da_lam_q2[j] * da_lam_k2[j]))
                   + lambda_init)
            qkv = _pre(h, m, norm1_g[i], da_w_qkv[j], None, all_rows, geo, tiled)
            q, k, v = _qk_prep(qkv, da_q_norm_g[j], da_k_norm_g[j], geo)
            mixed = _diff_attn(q, k, v, lam, da_subln_g[j], 1.0 - lambda_init, geo)
            w_out, b_out = da_w_o[j], None
        else:
            u = _pre(h, m, norm1_g[i], cf_w_pw1[j], cf_b_pw1[j], rows, geo, tiled, glu=True)
            mixed = _conf_conv(u, cf_dw_w[j], cf_dw_b[j], cf_ln_g[j], cf_ln_b[j], geo)
            w_out, b_out = cf_w_pw2[j], cf_b_pw2[j]
        h_mid, f, f_tiled = _post(mixed, w_out, b_out, h, m, norm2_g[i], rows, geo, tiled)
        h = _peer(f, f_tiled, h_mid, m, geo, peer_w_query[i], peer_sub_keys[i], peer_u[i], peer_v[i])
    return h[:lat_rows * LANE_TILES].reshape(bsz, n_lat, d)
```

```python
import functools
import math

import jax
import jax.numpy as jnp
from jax import lax
from jax.experimental import pallas as pl
from jax.experimental.pallas import tpu as pltpu

D_MODEL = 1024
DEPTH = 4
GRID_W = 64
N_MIXERS = 3
DA_HEADS = 8
DA_HEAD_DIM = D_MODEL // (2 * DA_HEADS)
DA_V_DIM = 2 * DA_HEAD_DIM
ROPE_FREQS = DA_HEAD_DIM // 4
ROPE_BASE = 10000.0
Q_BLOCK = 128
PEER_HEADS = 8
PEER_KEYS = 128
PEER_EXPERTS = PEER_KEYS * PEER_KEYS
PEER_QDIM = 256
PEER_HALF = PEER_QDIM // 2
PEER_TOPK = 16
PEER_HK = PEER_HEADS * PEER_TOPK
RMS_EPS = 1e-6
LN_EPS = 1e-5

LANES = 128
HALF_D = D_MODEL // 2
ROW_SUBLANES = HALF_D // LANES
PEER_TOKEN_BLOCK = 128
PEER_VMEM_LIMIT = 50 * 1024 * 1024
ROW_BLOCK = 256
DENSE_VMEM_LIMIT = 48 * 1024 * 1024


def _group_of(i, block_rows, geo):
    bsz, n_lat, _ = geo
    return jnp.minimum((i * block_rows) // n_lat, bsz)


def _seq_edges(i, block_rows, geo):
    bsz, n_lat, n_ctx = geo
    row0 = i * block_rows
    in_lat = row0 < bsz * n_lat
    pos = jnp.where(in_lat, row0 % n_lat, (row0 - bsz * n_lat) % n_ctx)
    length = jnp.where(in_lat, n_lat, n_ctx)
    return pos == 0, pos + block_rows == length


def _pack_table(tbl):
    bits = lax.bitcast_convert_type(tbl.astype(jnp.bfloat16), jnp.uint16).astype(jnp.uint32)
    words = bits[:, :HALF_D] | (bits[:, HALF_D:] << 16)
    return lax.bitcast_convert_type(words, jnp.int32).reshape(tbl.shape[0] * ROW_SUBLANES, LANES)


def _table_row(tbl_ref, first_sublane):
    return tbl_ref[pl.ds(pl.multiple_of(first_sublane, ROW_SUBLANES), ROW_SUBLANES), :]


def _unpack(w):
    lo = lax.bitcast_convert_type(w << 16, jnp.float32)
    hi = lax.bitcast_convert_type(w & jnp.int32(-65536), jnp.float32)
    return lo, hi


_G_STRIDE = PEER_HK + 1
IDX_OFFSETS = 8


def _token_pipeline(n_tokens, fill, drain, bufs, fill_leads):
    b_a, b_b, b_c, b_d = bufs
    fill(0, b_a)
    fill(1, b_b)

    def half(t, cur, nxt, last_half):
        ahead = [jnp.minimum(t + 2 + n, n_tokens - 1) if last_half else t + 2 + n for n in range(2)]
        fills = lambda: [fill(ahead[n], nxt[n]) for n in range(2)]
        drains = lambda: [drain(t + n, cur[n]) for n in range(2)]
        for emit in ((fills, drains) if fill_leads else (drains, fills)):
            emit()

    def token_quad(i, carry):
        half(4 * i, (b_a, b_b), (b_c, b_d), False)
        half(4 * i + 2, (b_c, b_d), (b_a, b_b), True)
        return carry

    lax.fori_loop(0, n_tokens // 4, token_quad, 0)


def _expert_rows(offs, idx_ref, tbl_ref, t):
    for c in range(PEER_HK // IDX_OFFSETS):
        rows = idx_ref.at[t, pl.ds(c * IDX_OFFSETS, IDX_OFFSETS)]
        for kk in range(IDX_OFFSETS):
            yield c * IDX_OFFSETS + kk, _table_row(tbl_ref, rows[offs[kk]])


def _pack_bf16_pairs(lo, hi):
    lo_bits = lax.bitcast_convert_type(lo.astype(jnp.bfloat16).astype(jnp.float32), jnp.int32)
    hi_bits = lax.bitcast_convert_type(hi.astype(jnp.bfloat16).astype(jnp.float32), jnp.int32)
    return (hi_bits & jnp.int32(-65536)) | lax.shift_right_logical(lo_bits, 16)


def _as_bf16_pairs(words):
    return pltpu.bitcast(words, jnp.bfloat16)


def _pair_product(words, pair_operand):
    return pltpu.bitcast(_as_bf16_pairs(words) * pair_operand, jnp.int32)


def _token_half(t, half):
    return pl.ds(pl.multiple_of(t * (2 * ROW_SUBLANES) + half * ROW_SUBLANES, ROW_SUBLANES), ROW_SUBLANES)


def _peer_z_kernel(offs_ref, idx_ref, f_ref, tbl_ref, z_ref, *bufs):
    offs = [offs_ref[i] for i in range(IDX_OFFSETS)]
    ones = jnp.ones((8, 2 * LANES), jnp.bfloat16)

    def products(t, p_scr):
        f_lo = f_ref[_token_half(t, 0), :]
        f_hi = f_ref[_token_half(t, 1), :]
        for k, w in _expert_rows(offs, idx_ref, tbl_ref, t):
            lo, hi = _unpack(w)
            p_scr[pl.ds(k, ROW_SUBLANES, stride=_G_STRIDE), :] = lo * f_lo + hi * f_hi

    def reduce(t, p_scr):
        ps = p_scr[pl.ds(0, PEER_HK), :]
        for s in range(1, ROW_SUBLANES):
            ps = ps + p_scr[pl.ds(s * _G_STRIDE, PEER_HK), :]
        ps_hi, ps_lo = _split_bf16(ps)
        z = lax.dot_general(ones, jnp.concatenate([ps_hi, ps_lo], axis=1), _NT, preferred_element_type=jnp.float32)
        z_ref[pl.ds(t, 1), :] = z[0:1]

    _token_pipeline(z_ref.shape[0], products, reduce, bufs, fill_leads=False)


def _peer_out_kernel(offs_ref, idx_ref, z_ref, gate_ref, tbl_ref, h_ref, m_ref, o_ref, act_ref, *bufs):
    offs = [offs_ref[i] for i in range(IDX_OFFSETS)]
    n_acc = 2
    z = z_ref[...]
    act = 0.5 * z * (1.0 + lax.erf(z * (2.0 ** -0.5))) * gate_ref[...]
    act_ref[...] = _pack_bf16_pairs(act, act)
    m_tile = jnp.concatenate([m_ref[0, 0, 0], m_ref[0, 0, 1]], axis=0)

    def spread(t, b_scr):
        b_scr[...] = jnp.broadcast_to(act_ref[pl.ds(t, 1), :], (PEER_HK, LANES)).T

    def accumulate(t, b_scr):
        acc_lo = [jnp.zeros((ROW_SUBLANES, LANES), jnp.float32) for _ in range(n_acc)]
        acc_hi = [jnp.zeros((ROW_SUBLANES, LANES), jnp.float32) for _ in range(n_acc)]
        for k, w in _expert_rows(offs, idx_ref, tbl_ref, t):
            a = _as_bf16_pairs(jnp.broadcast_to(b_scr[pl.ds(k, 1), :], (ROW_SUBLANES, LANES)))
            lo, hi = _unpack(_pair_product(w, a))
            acc_lo[k % n_acc] = acc_lo[k % n_acc] + lo
            acc_hi[k % n_acc] = acc_hi[k % n_acc] + hi
        total = jnp.concatenate([acc_lo[0] + acc_lo[1], acc_hi[0] + acc_hi[1]], axis=0)
        rows = pl.ds(pl.multiple_of(t * LANE_TILES, LANE_TILES), LANE_TILES)
        o_ref[rows, :] = h_ref[rows, :] + m_tile * total

    _token_pipeline(z_ref.shape[0], spread, accumulate, bufs, fill_leads=True)


def _table_spec(n_rows):
    return pl.BlockSpec((n_rows, LANES), lambda i: (0, 0), pipeline_mode=pl.Buffered(1))


def _split_bf16(x):
    hi = x.astype(jnp.bfloat16)
    return hi, (x - hi.astype(jnp.float32)).astype(jnp.bfloat16)


def _dot3(a_hi, a_lo, b_hi, b_lo, dims):
    dot = functools.partial(lax.dot_general, dimension_numbers=dims, preferred_element_type=jnp.float32)
    return dot(a_hi, b_hi) + (dot(a_lo, b_hi) + dot(a_hi, b_lo))


_NN = (((1,), (0,)), ((), ()))
_NT = (((1,), (1,)), ((), ()))
ROUTE_BLOCK = 512
ROUTE_SUB = LANES
ROUTE_LOCKSTEP = 4
_NEG = -jnp.inf


def _extract_top(scores, tag, n_top, out_refs, payloads=None):
    scores = list(scores)
    for r in range(n_top):
        for i, score in enumerate(scores):
            m = jnp.max(score, axis=0, keepdims=True)
            at = jnp.min(jnp.where(score == m, tag, 1e9), axis=0, keepdims=True)
            sel = tag == at
            out_refs[i][0][r:r + 1, :] = m
            if payloads is None:
                out_refs[i][1][r:r + 1, :] = at
            else:
                out_refs[i][1][r:r + 1, :] = jnp.max(jnp.where(sel, payloads[i], -1.0), axis=0, keepdims=True)
            scores[i] = jnp.where(sel, _NEG, score)


def _peer_route_kernel(f_ref, wq_hi_ref, wq_lo_ref, key_hi_ref, key_lo_ref, idx_ref, gate_ref,
                       q_scr, v_scr, i_scr, top_scr, e_scr, idx_t, gate_t):
    rb = f_ref.shape[0]
    f_hi, f_lo = _split_bf16(f_ref[...])
    for h in range(PEER_HEADS):
        cols = slice(h * PEER_QDIM, (h + 1) * PEER_QDIM)
        q = _dot3(f_hi, f_lo, wq_hi_ref[:, cols], wq_lo_ref[:, cols], _NN)
        q_scr[2 * h] = q[:, :PEER_HALF]
        q_scr[2 * h + 1] = q[:, PEER_HALF:]

    key_tag = lax.broadcasted_iota(jnp.int32, (PEER_KEYS, ROUTE_SUB), 0).astype(jnp.float32)
    sub8 = lax.broadcasted_iota(jnp.int32, (8, ROUTE_SUB), 0).astype(jnp.float32)
    sub16 = lax.broadcasted_iota(jnp.int32, (PEER_TOPK, ROUTE_SUB), 0).astype(jnp.float32)

    pos = [sub16] + [sub8 + float(a * PEER_TOPK) for a in range(1, 8)] + [(sub8 + 8.0) * PEER_TOPK]
    pos = jnp.concatenate(pos, axis=0)

    def candidates(h):
        v1, i1, v2, i2 = v_scr[2 * h], i_scr[2 * h], v_scr[2 * h + 1], i_scr[2 * h + 1]
        cand = [v1[0:1] + v2]
        exp_id = [i1[0:1] * PEER_KEYS + i2]
        for a in range(1, 8):
            n_b = PEER_TOPK // (a + 1)
            c = v1[a:a + 1] + v2[0:8]
            cand.append(c if n_b >= 8 else jnp.where(sub8 < n_b, c, _NEG))
            exp_id.append(i1[a:a + 1] * PEER_KEYS + i2[0:8])
        cand.append(v1[8:16] + v2[0:1])
        exp_id.append(i1[8:16] * PEER_KEYS + i2[0:1])
        return jnp.concatenate(cand, axis=0), jnp.concatenate(exp_id, axis=0)

    def token_tile(j, carry):
        tok = pl.ds(pl.multiple_of(j * ROUTE_SUB, ROUTE_SUB), ROUTE_SUB)

        def half_keys(grp, c):
            hps = [grp * ROUTE_LOCKSTEP + n for n in range(ROUTE_LOCKSTEP)]
            s_t = []
            for hp in hps:
                q_hi, q_lo = _split_bf16(q_scr[hp, tok, :])
                s_t.append(_dot3(key_hi_ref[hp], key_lo_ref[hp], q_hi, q_lo, _NT))
            _extract_top(s_t, key_tag, PEER_TOPK, [(v_scr.at[hp], i_scr.at[hp]) for hp in hps])
            return c

        lax.fori_loop(0, 2 * PEER_HEADS // ROUTE_LOCKSTEP, half_keys, 0)

        def products(grp, c):
            heads = [grp * ROUTE_LOCKSTEP + n for n in range(ROUTE_LOCKSTEP)]
            cands, exp_ids = zip(*(candidates(h) for h in heads))
            _extract_top(cands, pos, PEER_TOPK, [(top_scr.at[n], e_scr.at[n]) for n in range(ROUTE_LOCKSTEP)],
                         payloads=exp_ids)
            for n, h in enumerate(heads):
                top = top_scr[n]
                ex = jnp.exp(top - top[0:1])
                rows = pl.ds(pl.multiple_of(h * PEER_TOPK, PEER_TOPK), PEER_TOPK)
                gate_t[rows, :] = ex / jnp.sum(ex, axis=0, keepdims=True)
                idx_t[rows, :] = e_scr[n] * float(ROW_SUBLANES)
            return c

        lax.fori_loop(0, PEER_HEADS // ROUTE_LOCKSTEP, products, 0)
        idx_ref[tok, :] = idx_t[...].T.astype(jnp.int32)
        gate_ref[tok, :] = gate_t[...].T
        return carry

    lax.fori_loop(0, rb // ROUTE_SUB, token_tile, 0)


def _peer_route(h, wq_hi, wq_lo, key_hi, key_lo):
    n_tok = h.shape[0]
    rb = ROUTE_BLOCK
    full = lambda a: pl.BlockSpec(a.shape, lambda i: (0,) * a.ndim)
    t16 = pltpu.VMEM((ROUTE_LOCKSTEP, PEER_TOPK, ROUTE_SUB), jnp.float32)
    return pl.pallas_call(
        _peer_route_kernel,
        grid=(n_tok // rb,),
        in_specs=[pl.BlockSpec((rb, D_MODEL), lambda i: (i, 0)), full(wq_hi), full(wq_lo), full(key_hi), full(key_lo)],
        out_specs=[pl.BlockSpec((rb, PEER_HK), lambda i: (i, 0))] * 2,
        out_shape=[jax.ShapeDtypeStruct((n_tok, PEER_HK), jnp.int32),
                   jax.ShapeDtypeStruct((n_tok, PEER_HK), jnp.float32)],
        scratch_shapes=[pltpu.VMEM((2 * PEER_HEADS, rb, PEER_HALF), jnp.float32),
                        pltpu.VMEM((2 * PEER_HEADS, PEER_TOPK, ROUTE_SUB), jnp.float32),
                        pltpu.VMEM((2 * PEER_HEADS, PEER_TOPK, ROUTE_SUB), jnp.float32),
                        t16, t16,
                        pltpu.VMEM((PEER_HK, ROUTE_SUB), jnp.float32),
                        pltpu.VMEM((PEER_HK, ROUTE_SUB), jnp.float32)],
        compiler_params=pltpu.CompilerParams(dimension_semantics=("arbitrary",),
                                             vmem_limit_bytes=PEER_VMEM_LIMIT),
        name="peer_route",
    )(h, wq_hi, wq_lo, key_hi, key_lo)


def _peer_experts(f_tiled, idx, gates, tbl_u, tbl_v, h_tiled, m, geo):
    n_tok = idx.shape[0]
    n_exp = tbl_u.shape[0]
    tb = PEER_TOKEN_BLOCK
    grid = (n_tok // tb,)
    offs = jnp.arange(IDX_OFFSETS, dtype=jnp.int32)
    offs_spec = pl.BlockSpec(memory_space=pltpu.SMEM)
    smem_spec = pl.BlockSpec((tb, PEER_HK), lambda i: (i, 0), memory_space=pltpu.SMEM)
    gate_spec = pl.BlockSpec((1, 1, 2, ROW_SUBLANES, LANES), lambda i: (_group_of(i, tb, geo), 5, 0, 0, 0))
    vec_spec = pl.BlockSpec((tb, PEER_HK), lambda i: (i, 0))
    products = [pltpu.VMEM((ROW_SUBLANES * _G_STRIDE, LANES), jnp.float32)] * 4
    spreads = [pltpu.VMEM((PEER_HK, LANES), jnp.int32)] * 4
    params = pltpu.CompilerParams(dimension_semantics=("arbitrary",), vmem_limit_bytes=PEER_VMEM_LIMIT)
    z = pl.pallas_call(
        _peer_z_kernel,
        grid=grid,
        in_specs=[offs_spec, smem_spec, _tiled_spec(tb), _table_spec(n_exp)],
        out_specs=vec_spec,
        out_shape=jax.ShapeDtypeStruct((n_tok, PEER_HK), jnp.float32),
        scratch_shapes=products,
        compiler_params=params,
        name="peer_z",
    )(offs, idx, f_tiled, tbl_u)
    return pl.pallas_call(
        _peer_out_kernel,
        grid=grid,
        in_specs=[offs_spec, smem_spec, vec_spec, vec_spec, _table_spec(n_exp), _tiled_spec(tb), gate_spec],
        out_specs=_tiled_spec(tb),
        out_shape=jax.ShapeDtypeStruct((n_tok * LANE_TILES, LANES), jnp.float32),
        scratch_shapes=[pltpu.VMEM((tb, PEER_HK), jnp.int32)] + spreads,
        compiler_params=params,
        name="peer_out",
    )(offs, idx, z, gates, tbl_v, h_tiled, m.reshape(m.shape[0], 6, 2, ROW_SUBLANES, LANES))


def _peer(f, f_tiled, h_tiled, m, geo, w_query, sub_keys, expert_u, expert_v):
    wq_hi, wq_lo = _split_bf16(w_query)
    key_hi, key_lo = _split_bf16(sub_keys.reshape(2 * PEER_HEADS, PEER_KEYS, PEER_HALF))
    idx, gates = _peer_route(f, wq_hi, wq_lo, key_hi, key_lo)
    return _peer_experts(f_tiled, idx, gates, _pack_table(expert_u), _pack_table(expert_v), h_tiled, m, geo)


def _dense_params():
    return pltpu.CompilerParams(dimension_semantics=("arbitrary",), vmem_limit_bytes=DENSE_VMEM_LIMIT)


def _full_spec(a):
    return pl.BlockSpec(a.shape, lambda *_: (0,) * a.ndim)


def _mod_spec(block_rows, geo):
    return pl.BlockSpec((1, 6, D_MODEL), lambda i: (_group_of(i, block_rows, geo), 0, 0))


def _norm_mod(x, g, shift, scale):
    y = x * lax.rsqrt(jnp.mean(x * x, axis=-1, keepdims=True) + RMS_EPS)
    return (y * g) * (1.0 + scale) + shift


def _adaln_kernel(c_ref, w_ref, b_ref, o_ref):
    c = c_ref[...]
    s_hi, s_lo = _split_bf16(c * jax.nn.sigmoid(c))
    w_hi, w_lo = _split_bf16(w_ref[0])
    o_ref[0] = _dot3(s_hi, s_lo, w_hi, w_lo, _NN) + b_ref[0]


def _adaln(c, c_ctx, w_mod, b_mod):
    depth, d, n_out = w_mod.shape
    n_groups = c.shape[0] + 1
    pad = -n_groups % 8
    c_all = jnp.concatenate([c, c_ctx[None], jnp.zeros((pad, d), c.dtype)], axis=0)
    tn = D_MODEL
    out = pl.pallas_call(
        _adaln_kernel,
        grid=(depth, n_out // tn),
        in_specs=[pl.BlockSpec(c_all.shape, lambda l, n: (0, 0)),
                  pl.BlockSpec((1, d, tn), lambda l, n: (l, 0, n)),
                  pl.BlockSpec((1, 1, tn), lambda l, n: (l, 0, n))],
        out_specs=pl.BlockSpec((1, c_all.shape[0], tn), lambda l, n: (l, 0, n)),
        out_shape=jax.ShapeDtypeStruct((depth, c_all.shape[0], n_out), jnp.float32),
        compiler_params=pltpu.CompilerParams(dimension_semantics=("arbitrary", "arbitrary"),
                                             vmem_limit_bytes=DENSE_VMEM_LIMIT),
        name="adaln",
    )(c_all, w_mod, b_mod.reshape(depth, 1, n_out))
    return out[:, :n_groups].reshape(depth, n_groups, 6, D_MODEL)


LANE_TILES = D_MODEL // LANES


def _tiled_spec(block_tokens):
    return pl.BlockSpec((block_tokens * LANE_TILES, LANES), lambda i: (i, 0))


def _load_tokens(ref, tiled):
    if not tiled:
        return ref[...]
    n = ref.shape[0] // LANE_TILES
    return jnp.concatenate([ref[pl.ds(j, n, stride=LANE_TILES), :] for j in range(LANE_TILES)], axis=1)


def _store_tiled(ref, x):
    n = x.shape[0]
    for j in range(LANE_TILES):
        ref[pl.ds(j, n, stride=LANE_TILES), :] = x[:, j * LANES:(j + 1) * LANES]


def _pre_kernel(h_ref, m_ref, g_ref, w_ref, b_ref, o_ref, *, glu, tiled):
    a = _norm_mod(_load_tokens(h_ref, tiled), g_ref[...], m_ref[0, 0:1, :], m_ref[0, 1:2, :])
    y = jnp.dot(a.astype(jnp.bfloat16), w_ref[...], preferred_element_type=jnp.float32) + b_ref[...]
    if glu:
        half = y.shape[1] // 2
        y = y[:, :half] * jax.nn.sigmoid(y[:, half:])
    o_ref[...] = y


def _pre(h, m, g, w, b, rows, geo, tiled, glu=False):
    tm = ROW_BLOCK
    n_out = w.shape[1]
    b = jnp.zeros((n_out,), jnp.float32) if b is None else b
    width = n_out // 2 if glu else n_out
    g2, b2, w16 = g.reshape(1, -1), b.reshape(1, -1), w.astype(jnp.bfloat16)
    return pl.pallas_call(
        functools.partial(_pre_kernel, glu=glu, tiled=tiled),
        grid=(rows // tm,),
        in_specs=[_tiled_spec(tm) if tiled else pl.BlockSpec((tm, D_MODEL), lambda i: (i, 0)), _mod_spec(tm, geo),
                  _full_spec(g2), _full_spec(w16), _full_spec(b2)],
        out_specs=pl.BlockSpec((tm, width), lambda i: (i, 0)),
        out_shape=jax.ShapeDtypeStruct((rows, width), jnp.float32),
        compiler_params=_dense_params(),
        name="mixer_in",
    )(h, m, g2, w16, b2)


def _post_kernel(x_ref, w_ref, b_ref, h_ref, m_ref, g_ref, hn_ref, f_ref, ft_ref, *, tiled):
    y = jnp.dot(x_ref[...].astype(jnp.bfloat16), w_ref[...], preferred_element_type=jnp.float32) + b_ref[...]
    hn = _load_tokens(h_ref, tiled) + m_ref[0, 2:3, :] * y
    _store_tiled(hn_ref, hn)
    f = _norm_mod(hn, g_ref[...], m_ref[0, 3:4, :], m_ref[0, 4:5, :])
    f_ref[...] = f
    _store_tiled(ft_ref, f)


def _post(x, w, b, h, m, g, rows, geo, tiled):
    tm = ROW_BLOCK
    b = jnp.zeros((D_MODEL,), jnp.float32) if b is None else b
    g2, b2, w16 = g.reshape(1, -1), b.reshape(1, -1), w.astype(jnp.bfloat16)
    row_spec = pl.BlockSpec((tm, D_MODEL), lambda i: (i, 0))
    tiled_shape = jax.ShapeDtypeStruct((rows * LANE_TILES, LANES), jnp.float32)
    return pl.pallas_call(
        functools.partial(_post_kernel, tiled=tiled),
        grid=(rows // tm,),
        in_specs=[row_spec, _full_spec(w16), _full_spec(b2), _tiled_spec(tm) if tiled else row_spec,
                  _mod_spec(tm, geo), _full_spec(g2)],
        out_specs=[_tiled_spec(tm), row_spec, _tiled_spec(tm)],
        out_shape=[tiled_shape, jax.ShapeDtypeStruct((rows, D_MODEL), jnp.float32), tiled_shape],
        compiler_params=_dense_params(),
        name="mixer_out",
    )(x, w16, b2, h, m, g2)


def _halo_specs(tm, halo, width, n_rows):
    per = tm // halo
    last = n_rows // halo - 1
    prev_spec = pl.BlockSpec((halo, width), lambda i: (jnp.maximum(i * per - 1, 0), 0))
    next_spec = pl.BlockSpec((halo, width), lambda i: (jnp.minimum((i + 1) * per, last), 0))
    return prev_spec, next_spec


SC_HALO = 8
CF_HALO = 16
CF_WIDTH = 31


def _short_conv_kernel(y_ref, prev_ref, next_ref, w_ref, o_ref, *, geo):
    tm = y_ref.shape[0]
    d = D_MODEL
    first, last = _seq_edges(pl.program_id(0), tm, geo)
    v = y_ref[:, d:2 * d] * y_ref[:, 2 * d:]
    before = prev_ref[SC_HALO - 1:SC_HALO, d:2 * d] * prev_ref[SC_HALO - 1:SC_HALO, 2 * d:]
    after = next_ref[0:1, d:2 * d] * next_ref[0:1, 2 * d:]
    before = jnp.where(first, 0.0, before)
    after = jnp.where(last, 0.0, after)
    row = lax.broadcasted_iota(jnp.int32, (tm, 1), 0)
    up = jnp.where(row == 0, before, pltpu.roll(v, 1, axis=0))
    down = jnp.where(row == tm - 1, after, pltpu.roll(v, tm - 1, axis=0))
    conv = w_ref[0:1, :] * up + w_ref[1:2, :] * v + w_ref[2:3, :] * down
    o_ref[...] = y_ref[:, :d] * conv


def _short_conv(y3, conv_w, geo):
    rows = y3.shape[0]
    tm = ROW_BLOCK
    prev_spec, next_spec = _halo_specs(tm, SC_HALO, 3 * D_MODEL, rows)
    return pl.pallas_call(
        functools.partial(_short_conv_kernel, geo=geo),
        grid=(rows // tm,),
        in_specs=[pl.BlockSpec((tm, 3 * D_MODEL), lambda i: (i, 0)), prev_spec, next_spec, _full_spec(conv_w)],
        out_specs=pl.BlockSpec((tm, D_MODEL), lambda i: (i, 0)),
        out_shape=jax.ShapeDtypeStruct((rows, D_MODEL), jnp.float32),
        compiler_params=_dense_params(),
        name="short_conv",
    )(y3, y3, y3, conv_w)


def _conf_conv_kernel(u_ref, prev_ref, next_ref, w_ref, b_ref, g_ref, beta_ref, o_ref, pad_scr, *, geo):
    tm = u_ref.shape[0]
    first, last = _seq_edges(pl.program_id(0), tm, geo)
    pad_scr[0:CF_HALO, :] = jnp.where(first, 0.0, prev_ref[...])
    pad_scr[CF_HALO:CF_HALO + tm, :] = u_ref[...]
    pad_scr[CF_HALO + tm:, :] = jnp.where(last, 0.0, next_ref[...])
    shift = CF_HALO - CF_WIDTH // 2
    acc = w_ref[0:1, :] * pad_scr[shift:shift + tm, :]
    for j in range(1, CF_WIDTH):
        acc = acc + w_ref[j:j + 1, :] * pad_scr[shift + j:shift + j + tm, :]
    y = acc + b_ref[...]
    yc = y - jnp.mean(y, axis=-1, keepdims=True)
    ln = yc * lax.rsqrt(jnp.mean(yc * yc, axis=-1, keepdims=True) + LN_EPS) * g_ref[...] + beta_ref[...]
    o_ref[...] = ln * jax.nn.sigmoid(ln)


def _conf_conv(u, dw_w, dw_b, ln_g, ln_b, geo):
    rows = u.shape[0]
    tm = ROW_BLOCK
    prev_spec, next_spec = _halo_specs(tm, CF_HALO, D_MODEL, rows)
    vecs = [v.reshape(1, -1) for v in (dw_b, ln_g, ln_b)]
    return pl.pallas_call(
        functools.partial(_conf_conv_kernel, geo=geo),
        grid=(rows // tm,),
        in_specs=[pl.BlockSpec((tm, D_MODEL), lambda i: (i, 0)), prev_spec, next_spec, _full_spec(dw_w)]
        + [_full_spec(v) for v in vecs],
        out_specs=pl.BlockSpec((tm, D_MODEL), lambda i: (i, 0)),
        out_shape=jax.ShapeDtypeStruct((rows, D_MODEL), jnp.float32),
        scratch_shapes=[pltpu.VMEM((tm + 2 * CF_HALO, D_MODEL), jnp.float32)],
        compiler_params=_dense_params(),
        name="conformer_conv",
    )(u, u, u, dw_w, *vecs)


HEAD_W = 2 * DA_HEAD_DIM
ROPE_PAIR = ROPE_FREQS


def _rope_tables(n_lat):
    pos = jnp.arange(n_lat, dtype=jnp.float32)
    lane = jnp.arange(HEAD_W)
    within = lane % DA_HEAD_DIM
    coord = jnp.where((within // (2 * ROPE_FREQS) == 0)[None, :], (pos // GRID_W)[:, None], (pos % GRID_W)[:, None])
    inv_freq = ROPE_BASE ** (-(within % ROPE_FREQS).astype(jnp.float32) / ROPE_FREQS)
    ang = coord * inv_freq[None, :]
    first_of_pair = ((within // ROPE_FREQS) % 2 == 0)[None, :]
    sin = jnp.sin(ang)
    return jnp.cos(ang), jnp.where(first_of_pair, -sin, 0.0), jnp.where(first_of_pair, 0.0, sin)


def _qk_prep_kernel(qkv_ref, gq_ref, gk_ref, ones_ref, cos_ref, sa_ref, sb_ref, q_ref, k_ref, v_ref, *, geo):
    bsz, n_lat, _ = geo
    d = D_MODEL
    tm = qkv_ref.shape[0]
    is_lat = pl.program_id(0) * tm < bsz * n_lat
    cos = jnp.where(is_lat, cos_ref[...], 1.0)
    sin_a = jnp.where(is_lat, sa_ref[...], 0.0)
    sin_b = jnp.where(is_lat, sb_ref[...], 0.0)

    def prep(x, g, out_scale):
        sq_hi, sq_lo = _split_bf16(x * x)
        ssq = (jnp.dot(sq_hi, ones_ref[...], preferred_element_type=jnp.float32)
               + jnp.dot(sq_lo, ones_ref[...], preferred_element_type=jnp.float32))
        xn = x * lax.rsqrt(ssq * (1.0 / DA_HEAD_DIM) + RMS_EPS) * g
        heads = []
        for h in range(DA_HEADS):
            t = xn[:, h * HEAD_W:(h + 1) * HEAD_W]
            heads.append(t * cos + pltpu.roll(t, HEAD_W - ROPE_PAIR, axis=1) * sin_a
                         + pltpu.roll(t, ROPE_PAIR, axis=1) * sin_b)
        return (jnp.concatenate(heads, axis=1) * out_scale).astype(jnp.bfloat16)

    q_ref[...] = prep(qkv_ref[:, :d], gq_ref[...], DA_HEAD_DIM ** -0.5)
    k_ref[...] = prep(qkv_ref[:, d:2 * d], gk_ref[...], 1.0)
    v_ref[...] = qkv_ref[:, 2 * d:].astype(jnp.bfloat16)


def _qk_prep(qkv, q_norm_g, k_norm_g, geo):
    bsz, n_lat, _ = geo
    rows = qkv.shape[0]
    tm = ROW_BLOCK
    gq = jnp.tile(q_norm_g, D_MODEL // DA_HEAD_DIM).reshape(1, -1)
    gk = jnp.tile(k_norm_g, D_MODEL // DA_HEAD_DIM).reshape(1, -1)
    group = jnp.arange(D_MODEL) // DA_HEAD_DIM
    ones = (group[:, None] == group[None, :]).astype(jnp.bfloat16)
    tables = _rope_tables(n_lat)
    per_seq = n_lat // tm
    table_spec = pl.BlockSpec((tm, HEAD_W), lambda i: (i % per_seq, 0))
    row_spec = pl.BlockSpec((tm, D_MODEL), lambda i: (i, 0))
    return pl.pallas_call(
        functools.partial(_qk_prep_kernel, geo=geo),
        grid=(rows // tm,),
        in_specs=[pl.BlockSpec((tm, 3 * D_MODEL), lambda i: (i, 0)), _full_spec(gq), _full_spec(gk), _full_spec(ones)]
        + [table_spec] * 3,
        out_specs=[row_spec] * 3,
        out_shape=[jax.ShapeDtypeStruct((rows, D_MODEL), jnp.bfloat16)] * 3,
        compiler_params=_dense_params(),
        name="qk_prep",
    )(qkv, gq, gk, ones, *tables)


ATTN_Q_BLOCK = 512


def _diff_attn_kernel(lam_ref, q_ref, kl_ref, kc_ref, vl_ref, vc_ref, g_ref, o_ref, *, out_scale):
    q = q_ref[...]
    lane = lax.broadcasted_iota(jnp.int32, (1, HEAD_W), 1)
    zero = jnp.zeros((), q.dtype)

    def attend(qm):
        s_l = lax.dot_general(qm, kl_ref[...], _NT, preferred_element_type=jnp.float32)
        s_c = lax.dot_general(qm, kc_ref[...], _NT, preferred_element_type=jnp.float32)
        m = jnp.maximum(jnp.max(s_l, axis=-1, keepdims=True), jnp.max(s_c, axis=-1, keepdims=True))
        e_l = jnp.exp(s_l - m)
        e_c = jnp.exp(s_c - m)
        denom = jnp.sum(e_l, axis=-1, keepdims=True) + jnp.sum(e_c, axis=-1, keepdims=True)
        o = (jnp.dot(e_l.astype(jnp.bfloat16), vl_ref[...], preferred_element_type=jnp.float32)
             + jnp.dot(e_c.astype(jnp.bfloat16), vc_ref[...], preferred_element_type=jnp.float32))
        return o / denom

    o = attend(jnp.where(lane < DA_HEAD_DIM, q, zero)) - lam_ref[0] * attend(jnp.where(lane >= DA_HEAD_DIM, q, zero))
    o = o * lax.rsqrt(jnp.mean(o * o, axis=-1, keepdims=True) + RMS_EPS)
    o_ref[...] = o * g_ref[...] * out_scale


def _diff_attn(q, k, v, lam, subln_g, out_scale, geo):
    bsz, n_lat, n_ctx = geo
    tq = ATTN_Q_BLOCK
    nq = n_lat // tq
    ctx0 = bsz * n_lat // n_ctx
    q_spec = pl.BlockSpec((tq, HEAD_W), lambda b, h, i: (b * nq + i, h))
    lat_spec = pl.BlockSpec((n_lat, HEAD_W), lambda b, h, i: (b, h))
    ctx_spec = pl.BlockSpec((n_ctx, HEAD_W), lambda b, h, i: (ctx0 + b, h))
    g = subln_g.reshape(1, -1)
    return pl.pallas_call(
        functools.partial(_diff_attn_kernel, out_scale=out_scale),
        grid=(bsz, DA_HEADS, nq),
        in_specs=[pl.BlockSpec(memory_space=pltpu.SMEM), q_spec, lat_spec, ctx_spec, lat_spec, ctx_spec,
                  pl.BlockSpec(g.shape, lambda b, h, i: (0, 0))],
        out_specs=q_spec,
        out_shape=jax.ShapeDtypeStruct((bsz * n_lat, D_MODEL), jnp.float32),
        compiler_params=pltpu.CompilerParams(dimension_semantics=("arbitrary",) * 3,
                                             vmem_limit_bytes=DENSE_VMEM_LIMIT),
        name="diff_attn",
    )(lam.reshape(1), q, k, k, v, v, g)


def kernel(x, c, ctx, c_ctx, w_mod, b_mod, norm1_g, norm2_g, sc_w_in, sc_conv_w, sc_w_out, da_w_qkv, da_q_norm_g, da_k_norm_g, da_lam_q1, da_lam_k1, da_lam_q2, da_lam_k2, da_subln_g, da_w_o, cf_w_pw1, cf_b_pw1, cf_dw_w, cf_dw_b, cf_ln_g, cf_ln_b, cf_w_pw2, cf_b_pw2, peer_w_query, peer_sub_keys, peer_u, peer_v):
    bsz, n_lat, d = x.shape
    n_ctx = ctx.shape[1]
    geo = (bsz, n_lat, n_ctx)
    assert d == D_MODEL and n_lat % ATTN_Q_BLOCK == 0 and n_lat % ROW_BLOCK == 0 and n_ctx % ROW_BLOCK == 0
    assert (bsz * n_lat) % ROUTE_BLOCK == 0 and (bsz * n_ctx) % ROUTE_BLOCK == 0 and n_lat % GRID_W == 0
    lat_rows = bsz * n_lat
    all_rows = lat_rows + bsz * n_ctx
    ctx_read = [i for i in range(DEPTH) if i % N_MIXERS == 1]
    last_ctx_read = ctx_read[-1] if ctx_read else -1
    m_all = _adaln(c, c_ctx, w_mod, b_mod)
    h = jnp.concatenate([x.reshape(lat_rows, d), ctx.reshape(all_rows - lat_rows, d)], axis=0)
    for i in range(DEPTH):
        kind, j = i % N_MIXERS, i // N_MIXERS
        m = m_all[i]
        rows = all_rows if i < last_ctx_read else lat_rows
        tiled = i > 0
        if kind == 0:
            y3 = _pre(h, m, norm1_g[i], sc_w_in[j], None, rows, geo, tiled)
            mixed = _short_conv(y3, sc_conv_w[j], geo)
            w_out, b_out = sc_w_out[j], None
        elif kind == 1:
            assert rows == lat_rows
            lambda_init = 0.8 - 0.6 * math.exp(-0.3 * i)
            lam = (jnp.exp(jnp.sum(da_lam_q1[j] * da_lam_k1[j])) - jnp.exp(jnp.sum(da_lam_q2[j] * da_lam_k2[j]))
                   + lambda_init)
            qkv = _pre(h, m, norm1_g[i], da_w_qkv[j], None, all_rows, geo, tiled)
            q, k, v = _qk_prep(qkv, da_q_norm_g[j], da_k_norm_g[j], geo)
            mixed = _diff_attn(q, k, v, lam, da_subln_g[j], 1.0 - lambda_init, geo)
            w_out, b_out = da_w_o[j], None
        else:
            u = _pre(h, m, norm1_g[i], cf_w_pw1[j], cf_b_pw1[j], rows, geo, tiled, glu=True)
            mixed = _conf_conv(u, cf_dw_w[j], cf_dw_b[j], cf_ln_g[j], cf_ln_b[j], geo)
            w_out, b_out = cf_w_pw2[j], cf_b_pw2[j]
        h_mid, f, f_tiled = _post(mixed, w_out, b_out, h, m, norm2_g[i], rows, geo, tiled)
        h = _peer(f, f_tiled, h_mid, m, geo, peer_w_query[i], peer_sub_keys[i], peer_u[i], peer_v[i])
    return h[:lat_rows * LANE_TILES].reshape(bsz, n_lat, d)
```

```python
import functools
import math

import jax
import jax.numpy as jnp
from jax import lax
from jax.experimental import pallas as pl
from jax.experimental.pallas import tpu as pltpu

D_MODEL = 1024
DEPTH = 4
GRID_W = 64
N_MIXERS = 3
DA_HEADS = 8
DA_HEAD_DIM = D_MODEL // (2 * DA_HEADS)
DA_V_DIM = 2 * DA_HEAD_DIM
ROPE_FREQS = DA_HEAD_DIM // 4
ROPE_BASE = 10000.0
Q_BLOCK = 128
PEER_HEADS = 8
PEER_KEYS = 128
PEER_EXPERTS = PEER_KEYS * PEER_KEYS
PEER_QDIM = 256
PEER_HALF = PEER_QDIM // 2
PEER_TOPK = 16
PEER_HK = PEER_HEADS * PEER_TOPK
RMS_EPS = 1e-6
LN_EPS = 1e-5

LANES = 128
HALF_D = D_MODEL // 2
ROW_SUBLANES = HALF_D // LANES
PEER_TOKEN_BLOCK = 128
PEER_VMEM_LIMIT = 50 * 1024 * 1024
ROW_BLOCK = 256
DENSE_VMEM_LIMIT = 48 * 1024 * 1024


def _group_of(i, block_rows, geo):
    bsz, n_lat, _ = geo
    return jnp.minimum((i * block_rows) // n_lat, bsz)


def _seq_edges(i, block_rows, geo):
    bsz, n_lat, n_ctx = geo
    row0 = i * block_rows
    in_lat = row0 < bsz * n_lat
    pos = jnp.where(in_lat, row0 % n_lat, (row0 - bsz * n_lat) % n_ctx)
    length = jnp.where(in_lat, n_lat, n_ctx)
    return pos == 0, pos + block_rows == length


def _pack_table(tbl):
    bits = lax.bitcast_convert_type(tbl.astype(jnp.bfloat16), jnp.uint16).astype(jnp.uint32)
    words = bits[:, :HALF_D] | (bits[:, HALF_D:] << 16)
    return lax.bitcast_convert_type(words, jnp.int32).reshape(tbl.shape[0] * ROW_SUBLANES, LANES)


def _table_row(tbl_ref, first_sublane):
    return tbl_ref[pl.ds(pl.multiple_of(first_sublane, ROW_SUBLANES), ROW_SUBLANES), :]


def _unpack(w):
    lo = lax.bitcast_convert_type(w << 16, jnp.float32)
    hi = lax.bitcast_convert_type(w & jnp.int32(-65536), jnp.float32)
    return lo, hi


_G_STRIDE = PEER_HK + 1
IDX_OFFSETS = 8


def _token_pipeline(n_tokens, fill, drain, bufs, fill_leads):
    b_a, b_b, b_c, b_d = bufs
    fill(0, b_a)
    fill(1, b_b)

    def half(t, cur, nxt, last_half):
        ahead = [jnp.minimum(t + 2 + n, n_tokens - 1) if last_half else t + 2 + n for n in range(2)]
        fills = lambda: [fill(ahead[n], nxt[n]) for n in range(2)]
        drains = lambda: [drain(t + n, cur[n]) for n in range(2)]
        for emit in ((fills, drains) if fill_leads else (drains, fills)):
            emit()

    def token_quad(i, carry):
        half(4 * i, (b_a, b_b), (b_c, b_d), False)
        half(4 * i + 2, (b_c, b_d), (b_a, b_b), True)
        return carry

    lax.fori_loop(0, n_tokens // 4, token_quad, 0)


def _expert_rows(offs, idx_ref, tbl_ref, t):
    for c in range(PEER_HK // IDX_OFFSETS):
        rows = idx_ref.at[t, pl.ds(c * IDX_OFFSETS, IDX_OFFSETS)]
        for kk in range(IDX_OFFSETS):
            yield c * IDX_OFFSETS + kk, _table_row(tbl_ref, rows[offs[kk]])


def _pack_bf16_pairs(lo, hi):
    lo_bits = lax.bitcast_convert_type(lo.astype(jnp.bfloat16).astype(jnp.float32), jnp.int32)
    hi_bits = lax.bitcast_convert_type(hi.astype(jnp.bfloat16).astype(jnp.float32), jnp.int32)
    return (hi_bits & jnp.int32(-65536)) | lax.shift_right_logical(lo_bits, 16)


def _as_bf16_pairs(words):
    return pltpu.bitcast(words, jnp.bfloat16)


def _pair_product(words, pair_operand):
    return pltpu.bitcast(_as_bf16_pairs(words) * pair_operand, jnp.int32)


def _token_half(t, half):
    return pl.ds(pl.multiple_of(t * (2 * ROW_SUBLANES) + half * ROW_SUBLANES, ROW_SUBLANES), ROW_SUBLANES)


def _peer_z_kernel(offs_ref, idx_ref, f_ref, tbl_ref, z_ref, *bufs):
    offs = [offs_ref[i] for i in range(IDX_OFFSETS)]
    ones = jnp.ones((8, 2 * LANES), jnp.bfloat16)

    def products(t, p_scr):
        f_lo = f_ref[_token_half(t, 0), :]
        f_hi = f_ref[_token_half(t, 1), :]
        for k, w in _expert_rows(offs, idx_ref, tbl_ref, t):
            lo, hi = _unpack(w)
            p_scr[pl.ds(k, ROW_SUBLANES, stride=_G_STRIDE), :] = lo * f_lo + hi * f_hi

    def reduce(t, p_scr):
        ps = p_scr[pl.ds(0, PEER_HK), :]
        for s in range(1, ROW_SUBLANES):
            ps = ps + p_scr[pl.ds(s * _G_STRIDE, PEER_HK), :]
        ps_hi, ps_lo = _split_bf16(ps)
        z = lax.dot_general(ones, jnp.concatenate([ps_hi, ps_lo], axis=1), _NT, preferred_element_type=jnp.float32)
        z_ref[pl.ds(t, 1), :] = z[0:1]

    _token_pipeline(z_ref.shape[0], products, reduce, bufs, fill_leads=False)


def _peer_out_kernel(offs_ref, idx_ref, z_ref, gate_ref, tbl_ref, h_ref, m_ref, o_ref, act_ref, *bufs):
    offs = [offs_ref[i] for i in range(IDX_OFFSETS)]
    n_acc = 2
    z = z_ref[...]
    act = 0.5 * z * (1.0 + lax.erf(z * (2.0 ** -0.5))) * gate_ref[...]
    act_ref[...] = _pack_bf16_pairs(act, act)
    m_tile = jnp.concatenate([m_ref[0, 0, 0], m_ref[0, 0, 1]], axis=0)

    def spread(t, b_scr):
        b_scr[...] = jnp.broadcast_to(act_ref[pl.ds(t, 1), :], (PEER_HK, LANES)).T

    def accumulate(t, b_scr):
        acc_lo = [jnp.zeros((ROW_SUBLANES, LANES), jnp.float32) for _ in range(n_acc)]
        acc_hi = [jnp.zeros((ROW_SUBLANES, LANES), jnp.float32) for _ in range(n_acc)]
        for k, w in _expert_rows(offs, idx_ref, tbl_ref, t):
            a = _as_bf16_pairs(jnp.broadcast_to(b_scr[pl.ds(k, 1), :], (ROW_SUBLANES, LANES)))
            lo, hi = _unpack(_pair_product(w, a))
            acc_lo[k % n_acc] = acc_lo[k % n_acc] + lo
            acc_hi[k % n_acc] = acc_hi[k % n_acc] + hi
        total = jnp.concatenate([acc_lo[0] + acc_lo[1], acc_hi[0] + acc_hi[1]], axis=0)
        rows = pl.ds(pl.multiple_of(t * LANE_TILES, LANE_TILES), LANE_TILES)
        o_ref[rows, :] = h_ref[rows, :] + m_tile * total

    _token_pipeline(z_ref.shape[0], spread, accumulate, bufs, fill_leads=True)


def _table_spec(n_rows):
    return pl.BlockSpec((n_rows, LANES), lambda i: (0, 0), pipeline_mode=pl.Buffered(1))


def _split_bf16(x):
    hi = x.astype(jnp.bfloat16)
    return hi, (x - hi.astype(jnp.float32)).astype(jnp.bfloat16)


def _dot3(a_hi, a_lo, b_hi, b_lo, dims):
    dot = functools.partial(lax.dot_general, dimension_numbers=dims, preferred_element_type=jnp.float32)
    return dot(a_hi, b_hi) + (dot(a_lo, b_hi) + dot(a_hi, b_lo))


_NN = (((1,), (0,)), ((), ()))
_NT = (((1,), (1,)), ((), ()))
ROUTE_BLOCK = 512
ROUTE_SUB = LANES
ROUTE_LOCKSTEP = 4
ROUTE_KEY_LOCKSTEP = 8
_NEG = -jnp.inf


def _extract_top(scores, tag, n_top, out_refs, payloads=None):
    scores = list(scores)
    for r in range(n_top):
        for i, score in enumerate(scores):
            m = jnp.max(score, axis=0, keepdims=True)
            at = jnp.min(jnp.where(score == m, tag, 1e9), axis=0, keepdims=True)
            sel = tag == at
            out_refs[i][0][r:r + 1, :] = m
            if payloads is None:
                out_refs[i][1][r:r + 1, :] = at
            else:
                out_refs[i][1][r:r + 1, :] = jnp.max(jnp.where(sel, payloads[i], -1.0), axis=0, keepdims=True)
            scores[i] = jnp.where(sel, _NEG, score)


def _peer_route_kernel(f_ref, wq_hi_ref, wq_lo_ref, key_hi_ref, key_lo_ref, idx_ref, gate_ref,
                       q_scr, v_scr, i_scr, top_scr, e_scr, idx_t, gate_t):
    rb = f_ref.shape[0]
    f_hi, f_lo = _split_bf16(f_ref[...])
    for h in range(PEER_HEADS):
        cols = slice(h * PEER_QDIM, (h + 1) * PEER_QDIM)
        q = _dot3(f_hi, f_lo, wq_hi_ref[:, cols], wq_lo_ref[:, cols], _NN)
        q_scr[2 * h] = q[:, :PEER_HALF]
        q_scr[2 * h + 1] = q[:, PEER_HALF:]

    key_tag = lax.broadcasted_iota(jnp.int32, (PEER_KEYS, ROUTE_SUB), 0).astype(jnp.float32)
    sub8 = lax.broadcasted_iota(jnp.int32, (8, ROUTE_SUB), 0).astype(jnp.float32)
    sub16 = lax.broadcasted_iota(jnp.int32, (PEER_TOPK, ROUTE_SUB), 0).astype(jnp.float32)

    pos = [sub16] + [sub8 + float(a * PEER_TOPK) for a in range(1, 8)] + [(sub8 + 8.0) * PEER_TOPK]
    pos = jnp.concatenate(pos, axis=0)

    def candidates(h):
        v1, i1, v2, i2 = v_scr[2 * h], i_scr[2 * h], v_scr[2 * h + 1], i_scr[2 * h + 1]
        cand = [v1[0:1] + v2]
        exp_id = [i1[0:1] * PEER_KEYS + i2]
        for a in range(1, 8):
            n_b = PEER_TOPK // (a + 1)
            c = v1[a:a + 1] + v2[0:8]
            cand.append(c if n_b >= 8 else jnp.where(sub8 < n_b, c, _NEG))
            exp_id.append(i1[a:a + 1] * PEER_KEYS + i2[0:8])
        cand.append(v1[8:16] + v2[0:1])
        exp_id.append(i1[8:16] * PEER_KEYS + i2[0:1])
        return jnp.concatenate(cand, axis=0), jnp.concatenate(exp_id, axis=0)

    def token_tile(j, carry):
        tok = pl.ds(pl.multiple_of(j * ROUTE_SUB, ROUTE_SUB), ROUTE_SUB)

        def half_keys(grp, c):
            hps = [grp * ROUTE_KEY_LOCKSTEP + n for n in range(ROUTE_KEY_LOCKSTEP)]
            s_t = []
            for hp in hps:
                q_hi, q_lo = _split_bf16(q_scr[hp, tok, :])
                s_t.append(_dot3(key_hi_ref[hp], key_lo_ref[hp], q_hi, q_lo, _NT))
            _extract_top(s_t, key_tag, PEER_TOPK, [(v_scr.at[hp], i_scr.at[hp]) for hp in hps])
            return c

        lax.fori_loop(0, 2 * PEER_HEADS // ROUTE_KEY_LOCKSTEP, half_keys, 0)

        def products(grp, c):
            heads = [grp * ROUTE_LOCKSTEP + n for n in range(ROUTE_LOCKSTEP)]
            cands, exp_ids = zip(*(candidates(h) for h in heads))
            _extract_top(cands, pos, PEER_TOPK, [(top_scr.at[n], e_scr.at[n]) for n in range(ROUTE_LOCKSTEP)],
                         payloads=exp_ids)
            for n, h in enumerate(heads):
                top = top_scr[n]
                ex = jnp.exp(top - top[0:1])
                rows = pl.ds(pl.multiple_of(h * PEER_TOPK, PEER_TOPK), PEER_TOPK)
                gate_t[rows, :] = ex / jnp.sum(ex, axis=0, keepdims=True)
                idx_t[rows, :] = e_scr[n] * float(ROW_SUBLANES)
            return c

        lax.fori_loop(0, PEER_HEADS // ROUTE_LOCKSTEP, products, 0)
        idx_ref[tok, :] = idx_t[...].T.astype(jnp.int32)
        gate_ref[tok, :] = gate_t[...].T
        return carry

    lax.fori_loop(0, rb // ROUTE_SUB, token_tile, 0)


def _peer_route(h, wq_hi, wq_lo, key_hi, key_lo):
    n_tok = h.shape[0]
    rb = ROUTE_BLOCK
    full = lambda a: pl.BlockSpec(a.shape, lambda i: (0,) * a.ndim)
    t16 = pltpu.VMEM((ROUTE_LOCKSTEP, PEER_TOPK, ROUTE_SUB), jnp.float32)
    return pl.pallas_call(
        _peer_route_kernel,
        grid=(n_tok // rb,),
        in_specs=[pl.BlockSpec((rb, D_MODEL), lambda i: (i, 0)), full(wq_hi), full(wq_lo), full(key_hi), full(key_lo)],
        out_specs=[pl.BlockSpec((rb, PEER_HK), lambda i: (i, 0))] * 2,
        out_shape=[jax.ShapeDtypeStruct((n_tok, PEER_HK), jnp.int32),
                   jax.ShapeDtypeStruct((n_tok, PEER_HK), jnp.float32)],
        scratch_shapes=[pltpu.VMEM((2 * PEER_HEADS, rb, PEER_HALF), jnp.float32),
                        pltpu.VMEM((2 * PEER_HEADS, PEER_TOPK, ROUTE_SUB), jnp.float32),
                        pltpu.VMEM((2 * PEER_HEADS, PEER_TOPK, ROUTE_SUB), jnp.float32),
                        t16, t16,
                        pltpu.VMEM((PEER_HK, ROUTE_SUB), jnp.float32),
                        pltpu.VMEM((PEER_HK, ROUTE_SUB), jnp.float32)],
        compiler_params=pltpu.CompilerParams(dimension_semantics=("arbitrary",),
                                             vmem_limit_bytes=PEER_VMEM_LIMIT),
        name="peer_route",
    )(h, wq_hi, wq_lo, key_hi, key_lo)


def _peer_experts(f_tiled, idx, gates, tbl_u, tbl_v, h_tiled, m, geo):
    n_tok = idx.shape[0]
    n_exp = tbl_u.shape[0]
    tb = PEER_TOKEN_BLOCK
    grid = (n_tok // tb,)
    offs = jnp.arange(IDX_OFFSETS, dtype=jnp.int32)
    offs_spec = pl.BlockSpec(memory_space=pltpu.SMEM)
    smem_spec = pl.BlockSpec((tb, PEER_HK), lambda i: (i, 0), memory_space=pltpu.SMEM)
    gate_spec = pl.BlockSpec((1, 1, 2, ROW_SUBLANES, LANES), lambda i: (_group_of(i, tb, geo), 5, 0, 0, 0))
    vec_spec = pl.BlockSpec((tb, PEER_HK), lambda i: (i, 0))
    products = [pltpu.VMEM((ROW_SUBLANES * _G_STRIDE, LANES), jnp.float32)] * 4
    spreads = [pltpu.VMEM((PEER_HK, LANES), jnp.int32)] * 4
    params = pltpu.CompilerParams(dimension_semantics=("arbitrary",), vmem_limit_bytes=PEER_VMEM_LIMIT)
    z = pl.pallas_call(
        _peer_z_kernel,
        grid=grid,
        in_specs=[offs_spec, smem_spec, _tiled_spec(tb), _table_spec(n_exp)],
        out_specs=vec_spec,
        out_shape=jax.ShapeDtypeStruct((n_tok, PEER_HK), jnp.float32),
        scratch_shapes=products,
        compiler_params=params,
        name="peer_z",
    )(offs, idx, f_tiled, tbl_u)
    return pl.pallas_call(
        _peer_out_kernel,
        grid=grid,
        in_specs=[offs_spec, smem_spec, vec_spec, vec_spec, _table_spec(n_exp), _tiled_spec(tb), gate_spec],
        out_specs=_tiled_spec(tb),
        out_shape=jax.ShapeDtypeStruct((n_tok * LANE_TILES, LANES), jnp.float32),
        scratch_shapes=[pltpu.VMEM((tb, PEER_HK), jnp.int32)] + spreads,
        compiler_params=params,
        name="peer_out",
    )(offs, idx, z, gates, tbl_v, h_tiled, m.reshape(m.shape[0], 6, 2, ROW_SUBLANES, LANES))


def _peer(f, f_tiled, h_tiled, m, geo, w_query, sub_keys, expert_u, expert_v):
    wq_hi, wq_lo = _split_bf16(w_query)
    key_hi, key_lo = _split_bf16(sub_keys.reshape(2 * PEER_HEADS, PEER_KEYS, PEER_HALF))
    idx, gates = _peer_route(f, wq_hi, wq_lo, key_hi, key_lo)
    return _peer_experts(f_tiled, idx, gates, _pack_table(expert_u), _pack_table(expert_v), h_tiled, m, geo)


def _dense_params():
    return pltpu.CompilerParams(dimension_semantics=("arbitrary",), vmem_limit_bytes=DENSE_VMEM_LIMIT)


def _full_spec(a):
    return pl.BlockSpec(a.shape, lambda *_: (0,) * a.ndim)


def _mod_spec(block_rows, geo):
    return pl.BlockSpec((1, 6, D_MODEL), lambda i: (_group_of(i, block_rows, geo), 0, 0))


def _norm_mod(x, g, shift, scale):
    y = x * lax.rsqrt(jnp.mean(x * x, axis=-1, keepdims=True) + RMS_EPS)
    return (y * g) * (1.0 + scale) + shift


def _adaln_kernel(c_ref, w_ref, b_ref, o_ref):
    c = c_ref[...]
    s_hi, s_lo = _split_bf16(c * jax.nn.sigmoid(c))
    w_hi, w_lo = _split_bf16(w_ref[0])
    o_ref[0] = _dot3(s_hi, s_lo, w_hi, w_lo, _NN) + b_ref[0]


def _adaln(c, c_ctx, w_mod, b_mod):
    depth, d, n_out = w_mod.shape
    n_groups = c.shape[0] + 1
    pad = -n_groups % 8
    c_all = jnp.concatenate([c, c_ctx[None], jnp.zeros((pad, d), c.dtype)], axis=0)
    tn = D_MODEL
    out = pl.pallas_call(
        _adaln_kernel,
        grid=(depth, n_out // tn),
        in_specs=[pl.BlockSpec(c_all.shape, lambda l, n: (0, 0)),
                  pl.BlockSpec((1, d, tn), lambda l, n: (l, 0, n)),
                  pl.BlockSpec((1, 1, tn), lambda l, n: (l, 0, n))],
        out_specs=pl.BlockSpec((1, c_all.shape[0], tn), lambda l, n: (l, 0, n)),
        out_shape=jax.ShapeDtypeStruct((depth, c_all.shape[0], n_out), jnp.float32),
        compiler_params=pltpu.CompilerParams(dimension_semantics=("arbitrary", "arbitrary"),
                                             vmem_limit_bytes=DENSE_VMEM_LIMIT),
        name="adaln",
    )(c_all, w_mod, b_mod.reshape(depth, 1, n_out))
    return out[:, :n_groups].reshape(depth, n_groups, 6, D_MODEL)


LANE_TILES = D_MODEL // LANES


def _tiled_spec(block_tokens):
    return pl.BlockSpec((block_tokens * LANE_TILES, LANES), lambda i: (i, 0))


def _load_tokens(ref, tiled):
    if not tiled:
        return ref[...]
    n = ref.shape[0] // LANE_TILES
    return jnp.concatenate([ref[pl.ds(j, n, stride=LANE_TILES), :] for j in range(LANE_TILES)], axis=1)


def _store_tiled(ref, x):
    n = x.shape[0]
    for j in range(LANE_TILES):
        ref[pl.ds(j, n, stride=LANE_TILES), :] = x[:, j * LANES:(j + 1) * LANES]


def _pre_kernel(h_ref, m_ref, g_ref, w_ref, b_ref, o_ref, *, glu, tiled):
    a = _norm_mod(_load_tokens(h_ref, tiled), g_ref[...], m_ref[0, 0:1, :], m_ref[0, 1:2, :])
    y = jnp.dot(a.astype(jnp.bfloat16), w_ref[...], preferred_element_type=jnp.float32) + b_ref[...]
    if glu:
        half = y.shape[1] // 2
        y = y[:, :half] * jax.nn.sigmoid(y[:, half:])
    o_ref[...] = y


def _pre(h, m, g, w, b, rows, geo, tiled, glu=False):
    tm = ROW_BLOCK
    n_out = w.shape[1]
    b = jnp.zeros((n_out,), jnp.float32) if b is None else b
    width = n_out // 2 if glu else n_out
    g2, b2, w16 = g.reshape(1, -1), b.reshape(1, -1), w.astype(jnp.bfloat16)
    return pl.pallas_call(
        functools.partial(_pre_kernel, glu=glu, tiled=tiled),
        grid=(rows // tm,),
        in_specs=[_tiled_spec(tm) if tiled else pl.BlockSpec((tm, D_MODEL), lambda i: (i, 0)), _mod_spec(tm, geo),
                  _full_spec(g2), _full_spec(w16), _full_spec(b2)],
        out_specs=pl.BlockSpec((tm, width), lambda i: (i, 0)),
        out_shape=jax.ShapeDtypeStruct((rows, width), jnp.float32),
        compiler_params=_dense_params(),
        name="mixer_in",
    )(h, m, g2, w16, b2)


def _post_kernel(x_ref, w_ref, b_ref, h_ref, m_ref, g_ref, hn_ref, f_ref, ft_ref, *, tiled):
    y = jnp.dot(x_ref[...].astype(jnp.bfloat16), w_ref[...], preferred_element_type=jnp.float32) + b_ref[...]
    hn = _load_tokens(h_ref, tiled) + m_ref[0, 2:3, :] * y
    _store_tiled(hn_ref, hn)
    f = _norm_mod(hn, g_ref[...], m_ref[0, 3:4, :], m_ref[0, 4:5, :])
    f_ref[...] = f
    _store_tiled(ft_ref, f)


def _post(x, w, b, h, m, g, rows, geo, tiled):
    tm = ROW_BLOCK
    b = jnp.zeros((D_MODEL,), jnp.float32) if b is None else b
    g2, b2, w16 = g.reshape(1, -1), b.reshape(1, -1), w.astype(jnp.bfloat16)
    row_spec = pl.BlockSpec((tm, D_MODEL), lambda i: (i, 0))
    tiled_shape = jax.ShapeDtypeStruct((rows * LANE_TILES, LANES), jnp.float32)
    return pl.pallas_call(
        functools.partial(_post_kernel, tiled=tiled),
        grid=(rows // tm,),
        in_specs=[row_spec, _full_spec(w16), _full_spec(b2), _tiled_spec(tm) if tiled else row_spec,
                  _mod_spec(tm, geo), _full_spec(g2)],
        out_specs=[_tiled_spec(tm), row_spec, _tiled_spec(tm)],
        out_shape=[tiled_shape, jax.ShapeDtypeStruct((rows, D_MODEL), jnp.float32), tiled_shape],
        compiler_params=_dense_params(),
        name="mixer_out",
    )(x, w16, b2, h, m, g2)


def _halo_specs(tm, halo, width, n_rows):
    per = tm // halo
    last = n_rows // halo - 1
    prev_spec = pl.BlockSpec((halo, width), lambda i: (jnp.maximum(i * per - 1, 0), 0))
    next_spec = pl.BlockSpec((halo, width), lambda i: (jnp.minimum((i + 1) * per, last), 0))
    return prev_spec, next_spec


SC_HALO = 8
CF_HALO = 16
CF_WIDTH = 31


def _short_conv_kernel(y_ref, prev_ref, next_ref, w_ref, o_ref, *, geo):
    tm = y_ref.shape[0]
    d = D_MODEL
    first, last = _seq_edges(pl.program_id(0), tm, geo)
    v = y_ref[:, d:2 * d] * y_ref[:, 2 * d:]
    before = prev_ref[SC_HALO - 1:SC_HALO, d:2 * d] * prev_ref[SC_HALO - 1:SC_HALO, 2 * d:]
    after = next_ref[0:1, d:2 * d] * next_ref[0:1, 2 * d:]
    before = jnp.where(first, 0.0, before)
    after = jnp.where(last, 0.0, after)
    row = lax.broadcasted_iota(jnp.int32, (tm, 1), 0)
    up = jnp.where(row == 0, before, pltpu.roll(v, 1, axis=0))
    down = jnp.where(row == tm - 1, after, pltpu.roll(v, tm - 1, axis=0))
    conv = w_ref[0:1, :] * up + w_ref[1:2, :] * v + w_ref[2:3, :] * down
    o_ref[...] = y_ref[:, :d] * conv


def _short_conv(y3, conv_w, geo):
    rows = y3.shape[0]
    tm = ROW_BLOCK
    prev_spec, next_spec = _halo_specs(tm, SC_HALO, 3 * D_MODEL, rows)
    return pl.pallas_call(
        functools.partial(_short_conv_kernel, geo=geo),
        grid=(rows // tm,),
        in_specs=[pl.BlockSpec((tm, 3 * D_MODEL), lambda i: (i, 0)), prev_spec, next_spec, _full_spec(conv_w)],
        out_specs=pl.BlockSpec((tm, D_MODEL), lambda i: (i, 0)),
        out_shape=jax.ShapeDtypeStruct((rows, D_MODEL), jnp.float32),
        compiler_params=_dense_params(),
        name="short_conv",
    )(y3, y3, y3, conv_w)


def _conf_conv_kernel(u_ref, prev_ref, next_ref, w_ref, b_ref, g_ref, beta_ref, o_ref, pad_scr, *, geo):
    tm = u_ref.shape[0]
    first, last = _seq_edges(pl.program_id(0), tm, geo)
    pad_scr[0:CF_HALO, :] = jnp.where(first, 0.0, prev_ref[...])
    pad_scr[CF_HALO:CF_HALO + tm, :] = u_ref[...]
    pad_scr[CF_HALO + tm:, :] = jnp.where(last, 0.0, next_ref[...])
    shift = CF_HALO - CF_WIDTH // 2
    acc = w_ref[0:1, :] * pad_scr[shift:shift + tm, :]
    for j in range(1, CF_WIDTH):
        acc = acc + w_ref[j:j + 1, :] * pad_scr[shift + j:shift + j + tm, :]
    y = acc + b_ref[...]
    yc = y - jnp.mean(y, axis=-1, keepdims=True)
    ln = yc * lax.rsqrt(jnp.mean(yc * yc, axis=-1, keepdims=True) + LN_EPS) * g_ref[...] + beta_ref[...]
    o_ref[...] = ln * jax.nn.sigmoid(ln)


def _conf_conv(u, dw_w, dw_b, ln_g, ln_b, geo):
    rows = u.shape[0]
    tm = ROW_BLOCK
    prev_spec, next_spec = _halo_specs(tm, CF_HALO, D_MODEL, rows)
    vecs = [v.reshape(1, -1) for v in (dw_b, ln_g, ln_b)]
    return pl.pallas_call(
        functools.partial(_conf_conv_kernel, geo=geo),
        grid=(rows // tm,),
        in_specs=[pl.BlockSpec((tm, D_MODEL), lambda i: (i, 0)), prev_spec, next_spec, _full_spec(dw_w)]
        + [_full_spec(v) for v in vecs],
        out_specs=pl.BlockSpec((tm, D_MODEL), lambda i: (i, 0)),
        out_shape=jax.ShapeDtypeStruct((rows, D_MODEL), jnp.float32),
        scratch_shapes=[pltpu.VMEM((tm + 2 * CF_HALO, D_MODEL), jnp.float32)],
        compiler_params=_dense_params(),
        name="conformer_conv",
    )(u, u, u, dw_w, *vecs)


HEAD_W = 2 * DA_HEAD_DIM
ROPE_PAIR = ROPE_FREQS


def _rope_tables(n_lat):
    pos = jnp.arange(n_lat, dtype=jnp.float32)
    lane = jnp.arange(HEAD_W)
    within = lane % DA_HEAD_DIM
    coord = jnp.where((within // (2 * ROPE_FREQS) == 0)[None, :], (pos // GRID_W)[:, None], (pos % GRID_W)[:, None])
    inv_freq = ROPE_BASE ** (-(within % ROPE_FREQS).astype(jnp.float32) / ROPE_FREQS)
    ang = coord * inv_freq[None, :]
    first_of_pair = ((within // ROPE_FREQS) % 2 == 0)[None, :]
    sin = jnp.sin(ang)
    return jnp.cos(ang), jnp.where(first_of_pair, -sin, 0.0), jnp.where(first_of_pair, 0.0, sin)


def _qk_prep_kernel(qkv_ref, gq_ref, gk_ref, ones_ref, cos_ref, sa_ref, sb_ref, q_ref, k_ref, v_ref, *, geo):
    bsz, n_lat, _ = geo
    d = D_MODEL
    tm = qkv_ref.shape[0]
    is_lat = pl.program_id(0) * tm < bsz * n_lat
    cos = jnp.where(is_lat, cos_ref[...], 1.0)
    sin_a = jnp.where(is_lat, sa_ref[...], 0.0)
    sin_b = jnp.where(is_lat, sb_ref[...], 0.0)

    def prep(x, g, out_scale):
        sq_hi, sq_lo = _split_bf16(x * x)
        ssq = (jnp.dot(sq_hi, ones_ref[...], preferred_element_type=jnp.float32)
               + jnp.dot(sq_lo, ones_ref[...], preferred_element_type=jnp.float32))
        xn = x * lax.rsqrt(ssq * (1.0 / DA_HEAD_DIM) + RMS_EPS) * g
        heads = []
        for h in range(DA_HEADS):
            t = xn[:, h * HEAD_W:(h + 1) * HEAD_W]
            heads.append(t * cos + pltpu.roll(t, HEAD_W - ROPE_PAIR, axis=1) * sin_a
                         + pltpu.roll(t, ROPE_PAIR, axis=1) * sin_b)
        return (jnp.concatenate(heads, axis=1) * out_scale).astype(jnp.bfloat16)

    q_ref[...] = prep(qkv_ref[:, :d], gq_ref[...], DA_HEAD_DIM ** -0.5)
    k_ref[...] = prep(qkv_ref[:, d:2 * d], gk_ref[...], 1.0)
    v_ref[...] = qkv_ref[:, 2 * d:].astype(jnp.bfloat16)


def _qk_prep(qkv, q_norm_g, k_norm_g, geo):
    bsz, n_lat, _ = geo
    rows = qkv.shape[0]
    tm = ROW_BLOCK
    gq = jnp.tile(q_norm_g, D_MODEL // DA_HEAD_DIM).reshape(1, -1)
    gk = jnp.tile(k_norm_g, D_MODEL // DA_HEAD_DIM).reshape(1, -1)
    group = jnp.arange(D_MODEL) // DA_HEAD_DIM
    ones = (group[:, None] == group[None, :]).astype(jnp.bfloat16)
    tables = _rope_tables(n_lat)
    per_seq = n_lat // tm
    table_spec = pl.BlockSpec((tm, HEAD_W), lambda i: (i % per_seq, 0))
    row_spec = pl.BlockSpec((tm, D_MODEL), lambda i: (i, 0))
    return pl.pallas_call(
        functools.partial(_qk_prep_kernel, geo=geo),
        grid=(rows // tm,),
        in_specs=[pl.BlockSpec((tm, 3 * D_MODEL), lambda i: (i, 0)), _full_spec(gq), _full_spec(gk), _full_spec(ones)]
        + [table_spec] * 3,
        out_specs=[row_spec] * 3,
        out_shape=[jax.ShapeDtypeStruct((rows, D_MODEL), jnp.bfloat16)] * 3,
        compiler_params=_dense_params(),
        name="qk_prep",
    )(qkv, gq, gk, ones, *tables)


ATTN_Q_BLOCK = 512


def _diff_attn_kernel(lam_ref, q_ref, kl_ref, kc_ref, vl_ref, vc_ref, g_ref, o_ref, *, out_scale):
    q = q_ref[...]
    lane = lax.broadcasted_iota(jnp.int32, (1, HEAD_W), 1)
    zero = jnp.zeros((), q.dtype)

    def attend(qm):
        s_l = lax.dot_general(qm, kl_ref[...], _NT, preferred_element_type=jnp.float32)
        s_c = lax.dot_general(qm, kc_ref[...], _NT, preferred_element_type=jnp.float32)
        m = jnp.maximum(jnp.max(s_l, axis=-1, keepdims=True), jnp.max(s_c, axis=-1, keepdims=True))
        e_l = jnp.exp(s_l - m)
        e_c = jnp.exp(s_c - m)
        denom = jnp.sum(e_l, axis=-1, keepdims=True) + jnp.sum(e_c, axis=-1, keepdims=True)
        o = (jnp.dot(e_l.astype(jnp.bfloat16), vl_ref[...], preferred_element_type=jnp.float32)
             + jnp.dot(e_c.astype(jnp.bfloat16), vc_ref[...], preferred_element_type=jnp.float32))
        return o / denom

    o = attend(jnp.where(lane < DA_HEAD_DIM, q, zero)) - lam_ref[0] * attend(jnp.where(lane >= DA_HEAD_DIM, q, zero))
    o = o * lax.rsqrt(jnp.mean(o * o, axis=-1, keepdims=True) + RMS_EPS)
    o_ref[...] = o * g_ref[...] * out_scale


def _diff_attn(q, k, v, lam, subln_g, out_scale, geo):
    bsz, n_lat, n_ctx = geo
    tq = ATTN_Q_BLOCK
    nq = n_lat // tq
    ctx0 = bsz * n_lat // n_ctx
    q_spec = pl.BlockSpec((tq, HEAD_W), lambda b, h, i: (b * nq + i, h))
    lat_spec = pl.BlockSpec((n_lat, HEAD_W), lambda b, h, i: (b, h))
    ctx_spec = pl.BlockSpec((n_ctx, HEAD_W), lambda b, h, i: (ctx0 + b, h))
    g = subln_g.reshape(1, -1)
    return pl.pallas_call(
        functools.partial(_diff_attn_kernel, out_scale=out_scale),
        grid=(bsz, DA_HEADS, nq),
        in_specs=[pl.BlockSpec(memory_space=pltpu.SMEM), q_spec, lat_spec, ctx_spec, lat_spec, ctx_spec,
                  pl.BlockSpec(g.shape, lambda b, h, i: (0, 0))],
        out_specs=q_spec,
        out_shape=jax.ShapeDtypeStruct((bsz * n_lat, D_MODEL), jnp.float32),
        compiler_params=pltpu.CompilerParams(dimension_semantics=("arbitrary",) * 3,
                                             vmem_limit_bytes=DENSE_VMEM_LIMIT),
        name="diff_attn",
    )(lam.reshape(1), q, k, k, v, v, g)


def kernel(x, c, ctx, c_ctx, w_mod, b_mod, norm1_g, norm2_g, sc_w_in, sc_conv_w, sc_w_out, da_w_qkv, da_q_norm_g, da_k_norm_g, da_lam_q1, da_lam_k1, da_lam_q2, da_lam_k2, da_subln_g, da_w_o, cf_w_pw1, cf_b_pw1, cf_dw_w, cf_dw_b, cf_ln_g, cf_ln_b, cf_w_pw2, cf_b_pw2, peer_w_query, peer_sub_keys, peer_u, peer_v):
    bsz, n_lat, d = x.shape
    n_ctx = ctx.shape[1]
    geo = (bsz, n_lat, n_ctx)
    assert d == D_MODEL and n_lat % ATTN_Q_BLOCK == 0 and n_lat % ROW_BLOCK == 0 and n_ctx % ROW_BLOCK == 0
    assert (bsz * n_lat) % ROUTE_BLOCK == 0 and (bsz * n_ctx) % ROUTE_BLOCK == 0 and n_lat % GRID_W == 0
    lat_rows = bsz * n_lat
    all_rows = lat_rows + bsz * n_ctx
    ctx_read = [i for i in range(DEPTH) if i % N_MIXERS == 1]
    last_ctx_read = ctx_read[-1] if ctx_read else -1
    m_all = _adaln(c, c_ctx, w_mod, b_mod)
    h = jnp.concatenate([x.reshape(lat_rows, d), ctx.reshape(all_rows - lat_rows, d)], axis=0)
    for i in range(DEPTH):
        kind, j = i % N_MIXERS, i // N_MIXERS
        m = m_all[i]
        rows = all_rows if i < last_ctx_read else lat_rows
        tiled = i > 0
        if kind == 0:
            y3 = _pre(h, m, norm1_g[i], sc_w_in[j], None, rows, geo, tiled)
            mixed = _short_conv(y3, sc_conv_w[j], geo)
            w_out, b_out = sc_w_out[j], None
        elif kind == 1:
            assert rows == lat_rows
            lambda_init = 0.8 - 0.6 * math.exp(-0.3 * i)
            lam = (jnp.exp(jnp.sum(da_lam_q1[j] * da_lam_k1[j])) - jnp.exp(jnp.sum(da_lam_q2[j] * da_lam_k2[j]))
                   + lambda_init)
            qkv = _pre(h, m, norm1_g[i], da_w_qkv[j], None, all_rows, geo, tiled)
            q, k, v = _qk_prep(qkv, da_q_norm_g[j], da_k_norm_g[j], geo)
            mixed = _diff_attn(q, k, v, lam, da_subln_g[j], 1.0 - lambda_init, geo)
            w_out, b_out = da_w_o[j], None
        else:
            u = _pre(h, m, norm1_g[i], cf_w_pw1[j], cf_b_pw1[j], rows, geo, tiled, glu=True)
            mixed = _conf_conv(u, cf_dw_w[j], cf_dw_b[j], cf_ln_g[j], cf_ln_b[j], geo)
            w_out, b_out = cf_w_pw2[j], cf_b_pw2[j]
        h_mid, f, f_tiled = _post(mixed, w_out, b_out, h, m, norm2_g[i], rows, geo, tiled)
        h = _peer(f, f_tiled, h_mid, m, geo, peer_w_query[i], peer_sub_keys[i], peer_u[i], peer_v[i])
    return h[:lat_rows * LANE_TILES].reshape(bsz, n_lat, d)
```
